```python
import jax, jax.numpy as jnp
from jax import lax
import numpy as np

D_MODEL = 4096
BATCH = 2
SEQ = 8192
DEPTH = 1

MIX_WIDTH = D_MODEL
HEAD_DIM = 128
ATTN_WIDTH = MIX_WIDTH // 2
CONV_WIDTH = MIX_WIDTH - ATTN_WIDTH
N_Q_HEADS = ATTN_WIDTH // HEAD_DIM
N_KV_HEADS = 4
GQA_GROUP = N_Q_HEADS // N_KV_HEADS
KV_WIDTH = N_KV_HEADS * HEAD_DIM
CONV_K = 3
IN_COLS = ATTN_WIDTH + 2 * KV_WIDTH + 3 * CONV_WIDTH
SPLIT_POINTS = (ATTN_WIDTH,
                ATTN_WIDTH + KV_WIDTH,
                ATTN_WIDTH + 2 * KV_WIDTH,
                ATTN_WIDTH + 2 * KV_WIDTH + CONV_WIDTH,
                ATTN_WIDTH + 2 * KV_WIDTH + 2 * CONV_WIDTH)
GRID_W = 64
ROPE_THETA = 10000.0
Q_BLOCK = 128
N_KEYS = 128
N_EXPERTS = N_KEYS * N_KEYS
PEER_HEADS = 8
PEER_TOPK = 16
PEER_KEY_DIM = 128
PEER_QUERY_DIM = 2 * PEER_KEY_DIM
PEER_BLOCK = 128
EPS = 1e-6

kernel_name = "hymba_conv_axialgqa_peer_encoder_block"


def rms_norm(x, g):
    x32 = x.astype(jnp.float32)
    y = x32 * lax.rsqrt(jnp.mean(x32 * x32, axis=-1, keepdims=True) + EPS)
    return (y * g.astype(jnp.float32)).astype(x.dtype)


def axial_angles(seq_len):
    rows = seq_len // GRID_W
    row = jnp.repeat(jnp.arange(rows, dtype=jnp.float32), GRID_W)
    col = jnp.tile(jnp.arange(GRID_W, dtype=jnp.float32), rows)
    half = HEAD_DIM // 2
    inv = ROPE_THETA ** (-jnp.arange(0, half, 2, dtype=jnp.float32) / half)
    return row[:, None] * inv, col[:, None] * inv


def rotate(x, ang):
    cos = jnp.cos(ang)[:, None, :].astype(x.dtype)
    sin = jnp.sin(ang)[:, None, :].astype(x.dtype)
    x1, x2 = jnp.split(x, 2, axis=-1)
    return jnp.concatenate([x1 * cos - x2 * sin, x1 * sin + x2 * cos], axis=-1)


def apply_axial_rope(x, ang_row, ang_col):
    x_row, x_col = jnp.split(x, 2, axis=-1)
    return jnp.concatenate([rotate(x_row, ang_row), rotate(x_col, ang_col)], axis=-1)


def bidir_gqa(q, k, v):
    b, s, _, _ = q.shape
    nb = s // Q_BLOCK
    qb = q.reshape(b, nb, Q_BLOCK, N_KV_HEADS, GQA_GROUP, HEAD_DIM).transpose(1, 0, 2, 3, 4, 5)
    scale = HEAD_DIM ** -0.5

    def block(qi):
        sc = jnp.einsum('bqkgd,bskd->bkgqs', qi, k, preferred_element_type=jnp.float32) * scale
        p = jax.nn.softmax(sc, axis=-1).astype(v.dtype)
        return jnp.einsum('bkgqs,bskd->bqkgd', p, v)

    o = lax.map(block, qb)
    return o.transpose(1, 0, 2, 3, 4, 5).reshape(b, s, ATTN_WIDTH)


def short_conv(u, w):
    up = jnp.pad(u, ((0, 0), (1, 1), (0, 0)))
    return up[:, :-2] * w[0] + up[:, 1:-1] * w[1] + up[:, 2:] * w[2]


def hybrid_mixer(h, w_in, q_norm_g, k_norm_g, conv_w, attn_out_g, conv_out_g, w_out):
    b, s, _ = h.shape
    proj = h @ w_in
    q, k, v, cx, gb, gc = jnp.split(proj, SPLIT_POINTS, axis=-1)
    q = rms_norm(q.reshape(b, s, N_Q_HEADS, HEAD_DIM), q_norm_g)
    k = rms_norm(k.reshape(b, s, N_KV_HEADS, HEAD_DIM), k_norm_g)
    v = v.reshape(b, s, N_KV_HEADS, HEAD_DIM)
    ang_row, ang_col = axial_angles(s)
    q = apply_axial_rope(q, ang_row, ang_col)
    k = apply_axial_rope(k, ang_row, ang_col)
    attn = bidir_gqa(q, k, v)
    conv = gb * short_conv(gc * cx, conv_w)
    y = jnp.concatenate([rms_norm(attn, attn_out_g), rms_norm(conv, conv_out_g)], axis=-1)
    return y @ w_out


def peer_ffn(h, w_query, sub_keys, w_down, w_up):
    b, s, d = h.shape
    q = (h @ w_query).reshape(b, s, PEER_HEADS, 2, PEER_KEY_DIM)
    sc = jnp.einsum('bshpd,hpnd->bshpn', q, sub_keys)
    v_top, i_top = lax.top_k(sc, PEER_TOPK)
    cand_s = (v_top[..., 0, :, None] + v_top[..., 1, None, :]).reshape(
        b, s, PEER_HEADS, PEER_TOPK * PEER_TOPK)
    cand_i = (i_top[..., 0, :, None] * N_KEYS + i_top[..., 1, None, :]).reshape(
        b, s, PEER_HEADS, PEER_TOPK * PEER_TOPK)
    best_s, pos = lax.top_k(cand_s, PEER_TOPK)
    idx = jnp.take_along_axis(cand_i, pos, axis=-1)
    gate = jax.nn.softmax(best_s.astype(jnp.float32), axis=-1).astype(h.dtype)
    nblk = (b * s) // PEER_BLOCK
    xs = h.reshape(nblk, PEER_BLOCK, d)
    idx = idx.reshape(nblk, PEER_BLOCK, PEER_HEADS, PEER_TOPK)
    gate = gate.reshape(nblk, PEER_BLOCK, PEER_HEADS, PEER_TOPK)

    def block(args):
        xb, ib, gbk = args
        u = jnp.take(w_down, ib, axis=0)
        a = jax.nn.gelu(jnp.einsum('thkd,td->thk', u, xb), approximate=False) * gbk
        vv = jnp.take(w_up, ib, axis=0)
        return jnp.einsum('thk,thkd->td', a, vv)

    out = lax.map(block, (xs, idx, gate))
    return out.reshape(b, s, d)


def setup_inputs(seed: int = 0) -> dict:
    key = jax.random.key(seed)
    ks = jax.random.split(key, 16)
    f32 = jnp.float32

    def nrm(k, shape, scale):
        return jax.random.normal(k, shape, f32) * scale

    def gain(k, shape):
        return 1.0 + 0.02 * jax.random.normal(k, shape, f32)

    return {
        "x": nrm(ks[0], (BATCH, SEQ, D_MODEL), 1.0),
        "norm_mix_g": gain(ks[1], (DEPTH, D_MODEL)),
        "w_in": nrm(ks[2], (DEPTH, D_MODEL, IN_COLS), D_MODEL ** -0.5),
        "q_norm_g": gain(ks[3], (DEPTH, HEAD_DIM)),
        "k_norm_g": gain(ks[4], (DEPTH, HEAD_DIM)),
        "conv_w": nrm(ks[5], (DEPTH, CONV_K, CONV_WIDTH), CONV_K ** -0.5),
        "attn_out_g": gain(ks[6], (DEPTH, ATTN_WIDTH)),
        "conv_out_g": gain(ks[7], (DEPTH, CONV_WIDTH)),
        "w_out": nrm(ks[8], (DEPTH, MIX_WIDTH, D_MODEL), MIX_WIDTH ** -0.5),
        "norm_ffn_g": gain(ks[9], (DEPTH, D_MODEL)),
        "peer_w_query": nrm(ks[10], (DEPTH, D_MODEL, PEER_HEADS * PEER_QUERY_DIM), D_MODEL ** -0.5),
        "peer_sub_keys": nrm(ks[11], (DEPTH, PEER_HEADS, 2, N_KEYS, PEER_KEY_DIM), PEER_KEY_DIM ** -0.5),
        "peer_w_down": nrm(ks[12], (DEPTH, N_EXPERTS, D_MODEL), D_MODEL ** -0.5),
        "peer_w_up": nrm(ks[13], (DEPTH, N_EXPERTS, D_MODEL), 0.5),
        "norm_final_g": gain(ks[14], (D_MODEL,)),
    }


def reference(x, norm_mix_g, w_in, q_norm_g, k_norm_g, conv_w, attn_out_g, conv_out_g,
              w_out, norm_ffn_g, peer_w_query, peer_sub_keys, peer_w_down, peer_w_up,
              norm_final_g):
    for layer in range(DEPTH):
        h = rms_norm(x, norm_mix_g[layer])
        x = x + hybrid_mixer(h, w_in[layer], q_norm_g[layer], k_norm_g[layer], conv_w[layer],
                             attn_out_g[layer], conv_out_g[layer], w_out[layer])
        h = rms_norm(x, norm_ffn_g[layer])
        x = x + peer_ffn(h, peer_w_query[layer], peer_sub_keys[layer],
                         peer_w_down[layer], peer_w_up[layer])
    return rms_norm(x, norm_final_g)
```

```python
import functools

import numpy as np
import jax
import jax.numpy as jnp
from jax import lax
from jax.experimental import pallas as pl
from jax.experimental.pallas import tpu as pltpu

F32 = jnp.float32
BF16 = jnp.bfloat16

HEAD_DIM = 128
N_Q_HEADS = 16
N_KV_HEADS = 4
GQA_GROUP = N_Q_HEADS // N_KV_HEADS
ATTN_WIDTH = N_Q_HEADS * HEAD_DIM
KV_WIDTH = N_KV_HEADS * HEAD_DIM
CONV_WIDTH = 2048
GRID_W = 64
ROPE_THETA = 10000.0
N_KEYS = 128
PEER_HEADS = 8
PEER_TOPK = 16
EPS = 1e-6
INV_SQRT2 = 0.7071067811865476

LANES = 128
BF16_SUBLANES = 16
VMEM_LIMIT_BYTES = 56 * 1024 * 1024

COL_Q = 0
COL_CX = ATTN_WIDTH
COL_GB = COL_CX + CONV_WIDTH
COL_GC = COL_GB + CONV_WIDTH
COL_K = COL_GC + CONV_WIDTH
COL_V = COL_K + KV_WIDTH
IN_COLS = COL_V + KV_WIDTH


def _params(*sem):
    return pltpu.CompilerParams(dimension_semantics=sem, vmem_limit_bytes=VMEM_LIMIT_BYTES)


def _rmsnorm_kernel(x_ref, g_ref, o_ref, *, transpose):
    x = x_ref[...]
    ms = jnp.mean(x * x, axis=-1, keepdims=True)
    y = x * lax.rsqrt(ms + EPS) * g_ref[...]
    if transpose:
        y = y.T
    o_ref[...] = y.astype(o_ref.dtype)


def _rmsnorm(x, g, *, tm, transpose=False):
    t, d = x.shape
    if transpose:
        out_shape = jax.ShapeDtypeStruct((d, t), BF16)
        out_spec = pl.BlockSpec((d, tm), lambda i: (0, i))
    else:
        out_shape = jax.ShapeDtypeStruct((t, d), BF16)
        out_spec = pl.BlockSpec((tm, d), lambda i: (i, 0))
    return pl.pallas_call(
        functools.partial(_rmsnorm_kernel, transpose=transpose),
        grid=(t // tm,),
        in_specs=[pl.BlockSpec((tm, d), lambda i: (i, 0)), pl.BlockSpec((1, d), lambda i: (0, 0))],
        out_specs=out_spec,
        out_shape=out_shape,
        compiler_params=_params("parallel"),
        name="rmsnorm_t" if transpose else "rmsnorm",
    )(x, g.reshape(1, d))


def _mm_kernel(a_ref, b_ref, o_ref):
    o_ref[...] = jnp.dot(a_ref[...], b_ref[...], preferred_element_type=F32).astype(o_ref.dtype)


def _mm_res_kernel(a_ref, b_ref, r_ref, o_ref):
    acc = jnp.dot(a_ref[...], b_ref[...], preferred_element_type=F32)
    o_ref[...] = (r_ref[...] + acc).astype(o_ref.dtype)


def _matmul(a, b, *, tm, tn, out_dtype, residual=None, name):
    m, k = a.shape
    _, n = b.shape
    in_specs = [pl.BlockSpec((tm, k), lambda i, j: (i, 0)), pl.BlockSpec((k, tn), lambda i, j: (0, j))]
    args = [a, b]
    body = _mm_kernel
    if residual is not None:
        in_specs.append(pl.BlockSpec((tm, tn), lambda i, j: (i, j)))
        args.append(residual)
        body = _mm_res_kernel
    return pl.pallas_call(
        body,
        grid=(m // tm, n // tn),
        in_specs=in_specs,
        out_specs=pl.BlockSpec((tm, tn), lambda i, j: (i, j)),
        out_shape=jax.ShapeDtypeStruct((m, n), out_dtype),
        compiler_params=_params("parallel", "parallel"),
        name=name,
    )(*args)


def _rope_tables(seq):
    t = np.arange(seq)
    half = HEAD_DIM // 2
    inv = ROPE_THETA ** (-np.arange(0, half, 2, dtype=np.float64) / half)
    ang_row = (t // GRID_W)[:, None] * inv
    ang_col = (t % GRID_W)[:, None] * inv
    zeros = np.zeros_like(ang_row)
    cos = np.concatenate([np.cos(ang_row)] * 2 + [np.cos(ang_col)] * 2, axis=1)
    sa = np.concatenate([-np.sin(ang_row), zeros, -np.sin(ang_col), zeros], axis=1)
    sb = np.concatenate([zeros, np.sin(ang_row), zeros, np.sin(ang_col)], axis=1)
    return (jnp.asarray(cos, F32), jnp.asarray(sa, F32), jnp.asarray(sb, F32))


def _qk_kernel(q_ref, k_ref, c_ref, sa_ref, sb_ref, gq_ref, gk_ref, qo_ref, ko_ref):
    c, sa, sb = c_ref[...], sa_ref[...], sb_ref[...]
    quarter = HEAD_DIM // 4

    def prep(x, g, scale):
        ms = jnp.mean(x * x, axis=-1, keepdims=True)
        y = x * lax.rsqrt(ms + EPS) * g
        r = y * c + pltpu.roll(y, HEAD_DIM - quarter, 1) * sa + pltpu.roll(y, quarter, 1) * sb
        return r * scale

    for h in range(N_Q_HEADS):
        sl = slice(h * HEAD_DIM, (h + 1) * HEAD_DIM)
        qo_ref[:, sl] = prep(q_ref[:, sl].astype(F32), gq_ref[...], HEAD_DIM ** -0.5).astype(qo_ref.dtype)
    for h in range(N_KV_HEADS):
        sl = slice(h * HEAD_DIM, (h + 1) * HEAD_DIM)
        ko_ref[:, sl] = prep(k_ref[:, sl].astype(F32), gk_ref[...], 1.0).astype(ko_ref.dtype)


def _qk_prep(proj, gq, gk, seq, *, tm):
    t = proj.shape[0]
    tables = _rope_tables(seq)
    per_seq = seq // tm
    tab_spec = pl.BlockSpec((tm, HEAD_DIM), lambda i: (i % per_seq, 0))
    g_spec = pl.BlockSpec((1, HEAD_DIM), lambda i: (0, 0))
    return pl.pallas_call(
        _qk_kernel,
        grid=(t // tm,),
        in_specs=[
            pl.BlockSpec((tm, ATTN_WIDTH), lambda i: (i, COL_Q // ATTN_WIDTH)),
            pl.BlockSpec((tm, KV_WIDTH), lambda i: (i, COL_K // KV_WIDTH)),
            tab_spec, tab_spec, tab_spec, g_spec, g_spec,
        ],
        out_specs=[pl.BlockSpec((tm, ATTN_WIDTH), lambda i: (i, 0)), pl.BlockSpec((tm, KV_WIDTH), lambda i: (i, 0))],
        out_shape=[jax.ShapeDtypeStruct((t, ATTN_WIDTH), BF16), jax.ShapeDtypeStruct((t, KV_WIDTH), BF16)],
        compiler_params=_params("parallel"),
        name="qk_prep",
    )(proj, proj, *tables, gq.reshape(1, HEAD_DIM), gk.reshape(1, HEAD_DIM))


def _attn_kernel(q_ref, k_ref, v_ref, o_ref, *, tk):
    tq = q_ref.shape[0]
    n_chunks = k_ref.shape[0] // tk
    for h in range(GQA_GROUP):
        sl = slice(h * HEAD_DIM, (h + 1) * HEAD_DIM)
        q = q_ref[:, sl]

        def body(i, carry, q=q):
            m, l, acc = carry
            off = pl.multiple_of(i * tk, tk)
            kc = k_ref[pl.ds(off, tk), :]
            vc = v_ref[pl.ds(off, tk), :]
            s = lax.dot_general(q, kc, (((1,), (1,)), ((), ())), preferred_element_type=F32)
            m_new = jnp.maximum(m, jnp.max(s, axis=-1, keepdims=True))
            alpha = jnp.exp(m - m_new)
            p = jnp.exp(s - m_new)
            l = alpha * l + jnp.sum(p, axis=-1, keepdims=True)
            acc = alpha * acc + jnp.dot(p.astype(BF16), vc, preferred_element_type=F32)
            return m_new, l, acc

        init = (jnp.full((tq, 1), -jnp.inf, F32), jnp.zeros((tq, 1), F32), jnp.zeros((tq, HEAD_DIM), F32))
        _, l, acc = lax.fori_loop(0, n_chunks, body, init)
        o_ref[:, sl] = (acc / l).astype(o_ref.dtype)


def _attention(q, k, proj, batch, seq, *, tq, tk):
    t = q.shape[0]
    nq = seq // tq
    group_w = GQA_GROUP * HEAD_DIM
    return pl.pallas_call(
        functools.partial(_attn_kernel, tk=tk),
        grid=(batch, N_KV_HEADS, nq),
        in_specs=[
            pl.BlockSpec((tq, group_w), lambda b, g, i: (b * nq + i, g)),
            pl.BlockSpec((seq, HEAD_DIM), lambda b, g, i: (b, g)),
            pl.BlockSpec((seq, HEAD_DIM), lambda b, g, i: (b, COL_V // HEAD_DIM + g)),
        ],
        out_specs=pl.BlockSpec((tq, group_w), lambda b, g, i: (b * nq + i, g)),
        out_shape=jax.ShapeDtypeStruct((t, ATTN_WIDTH), BF16),
        compiler_params=_params("parallel", "parallel", "parallel"),
        name="attention",
    )(q, k, proj)


def _mix_kernel(attn_ref, cx_ref, gb_ref, gc_ref, cxp_ref, gcp_ref, cxn_ref, gcn_ref, w_ref, ga_ref, gv_ref,
                o_ref, *, per_seq):
    i = pl.program_id(0)
    tm = cx_ref.shape[0]
    u = gc_ref[...].astype(F32) * cx_ref[...].astype(F32)
    last_row = BF16_SUBLANES - 1
    prev = gcp_ref[last_row:, :].astype(F32) * cxp_ref[last_row:, :].astype(F32)
    nxt = gcn_ref[0:1, :].astype(F32) * cxn_ref[0:1, :].astype(F32)
    pos = i % per_seq
    prev = jnp.where(pos == 0, 0.0, prev)
    nxt = jnp.where(pos == per_seq - 1, 0.0, nxt)
    row = lax.broadcasted_iota(jnp.int32, (tm, 1), 0)
    u_dn = jnp.where(row == 0, prev, pltpu.roll(u, 1, 0))
    u_up = jnp.where(row == tm - 1, nxt, pltpu.roll(u, tm - 1, 0))
    w = w_ref[...]
    conv = gb_ref[...].astype(F32) * (u_dn * w[0:1] + u * w[1:2] + u_up * w[2:3])

    def norm(z, g):
        ms = jnp.mean(z * z, axis=-1, keepdims=True)
        return z * lax.rsqrt(ms + EPS) * g

    o_ref[:, :ATTN_WIDTH] = norm(attn_ref[...].astype(F32), ga_ref[...]).astype(o_ref.dtype)
    o_ref[:, ATTN_WIDTH:] = norm(conv, gv_ref[...]).astype(o_ref.dtype)


def _mix_norm(attn, proj, conv_w, ga, gv, seq, *, tm):
    t = attn.shape[0]
    per_seq = seq // tm
    halo = BF16_SUBLANES
    n_halo = t // halo
    cw = CONV_WIDTH

    def main(col):
        return pl.BlockSpec((tm, cw), lambda i: (i, col // cw))

    def prev(col):
        return pl.BlockSpec((halo, cw), lambda i: (jnp.maximum(i * (tm // halo) - 1, 0), col // cw))

    def nxt(col):
        return pl.BlockSpec((halo, cw), lambda i: (jnp.minimum((i + 1) * (tm // halo), n_halo - 1), col // cw))

    def vec(n):
        return pl.BlockSpec((n, cw), lambda i: (0, 0))

    return pl.pallas_call(
        functools.partial(_mix_kernel, per_seq=per_seq),
        grid=(t // tm,),
        in_specs=[pl.BlockSpec((tm, ATTN_WIDTH), lambda i: (i, 0)), main(COL_CX), main(COL_GB), main(COL_GC),
                  prev(COL_CX), prev(COL_GC), nxt(COL_CX), nxt(COL_GC), vec(3), vec(1), vec(1)],
        out_specs=pl.BlockSpec((tm, ATTN_WIDTH + cw), lambda i: (i, 0)),
        out_shape=jax.ShapeDtypeStruct((t, ATTN_WIDTH + cw), BF16),
        compiler_params=_params("parallel"),
        name="mix_norm",
    )(attn, proj, proj, proj, proj, proj, proj, proj, conv_w, ga.reshape(1, -1), gv.reshape(1, -1))


def _peer_scores_kernel(wq_ref, h_ref, sk_ref, o_ref):
    qt = jnp.dot(wq_ref[...], h_ref[...], preferred_element_type=F32)
    for p in range(2):
        rows = slice(p * N_KEYS, (p + 1) * N_KEYS)
        o_ref[rows, :] = jnp.dot(sk_ref[0, p], qt[rows, :], preferred_element_type=F32,
                                 precision=lax.Precision.HIGHEST)


def _peer_scores(wq_t, h_t, sub_keys, *, tn):
    rows, d = wq_t.shape
    t = h_t.shape[1]
    per_head = 2 * N_KEYS
    return pl.pallas_call(
        _peer_scores_kernel,
        grid=(t // tn, PEER_HEADS),
        in_specs=[
            pl.BlockSpec((per_head, d), lambda i, h: (h, 0)),
            pl.BlockSpec((d, tn), lambda i, h: (0, i)),
            pl.BlockSpec((1, 2, N_KEYS, N_KEYS), lambda i, h: (h, 0, 0, 0)),
        ],
        out_specs=pl.BlockSpec((per_head, tn), lambda i, h: (h, i)),
        out_shape=jax.ShapeDtypeStruct((rows, t), F32),
        compiler_params=_params("parallel", "parallel"),
        name="peer_scores",
    )(wq_t, h_t, sub_keys)


def _top_sorted(work, n):
    out = []
    for _ in range(n):
        m = jnp.max(work, axis=0, keepdims=True)
        out.append(m)
        work = jnp.where(work == m, -jnp.inf, work)
    return out


def _peer_select_kernel(sc_ref, ea_ref, eb_ref, tau_ref, a_top, b_top):
    k = PEER_TOPK
    sub = 8
    for h in range(PEER_HEADS):
        a = sc_ref[h * 2 * N_KEYS:h * 2 * N_KEYS + N_KEYS, :]
        b = sc_ref[h * 2 * N_KEYS + N_KEYS:(h + 1) * 2 * N_KEYS, :]
        a_sorted = _top_sorted(a, k)
        b_sorted = _top_sorted(b, k)
        for r in range(k):
            a_top[r:r + 1, :] = a_sorted[r]
            b_top[r:r + 1, :] = b_sorted[r]
        slabs = [a_sorted[0] + b_top[...]]
        for r in range(1, sub):
            slabs.append(a_sorted[r] + b_top[0:sub, :])
        slabs.append(a_top[sub:k, :] + b_sorted[0])
        cand = jnp.concatenate(slabs, axis=0)
        top = _top_sorted(cand, k)
        z = jnp.ones_like(top[0])
        for r in range(1, k):
            z = z + jnp.exp(top[r] - top[0])
        ea_ref[h * N_KEYS:(h + 1) * N_KEYS, :] = jnp.exp(a - a_sorted[0])
        eb_ref[h * N_KEYS:(h + 1) * N_KEYS, :] = jnp.exp(b - b_sorted[0]) / z
        tau_ref[h:h + 1, :] = top[k - 1]


def _peer_select(sc_t, *, tn):
    rows, t = sc_t.shape
    half = rows // 2
    return pl.pallas_call(
        _peer_select_kernel,
        grid=(t // tn,),
        in_specs=[pl.BlockSpec((rows, tn), lambda i: (0, i))],
        out_specs=[pl.BlockSpec((half, tn), lambda i: (0, i)), pl.BlockSpec((half, tn), lambda i: (0, i)),
                   pl.BlockSpec((PEER_HEADS, tn), lambda i: (0, i))],
        out_shape=[jax.ShapeDtypeStruct((half, t), F32), jax.ShapeDtypeStruct((half, t), F32),
                   jax.ShapeDtypeStruct((PEER_HEADS, t), F32)],
        scratch_shapes=[pltpu.VMEM((PEER_TOPK, tn), F32), pltpu.VMEM((PEER_TOPK, tn), F32)],
        compiler_params=_params("parallel"),
        name="peer_select",
    )(sc_t)


def _peer_kernel(h_ref, wd_ref, wu_ref, sc_ref, ea_ref, eb_ref, tau_ref, o_ref):
    j = pl.program_id(1)
    n_slabs = wd_ref.shape[0] // N_KEYS

    @pl.when(j == 0)
    def _():
        o_ref[...] = jnp.zeros_like(o_ref)

    s_t = jnp.dot(wd_ref[...], h_ref[...], preferred_element_type=F32)
    slabs = []
    for ii in range(n_slabs):
        ig = j * n_slabs + ii
        s = s_t[ii * N_KEYS:(ii + 1) * N_KEYS, :]
        act = 0.5 * s * (1.0 + lax.erf(s * INV_SQRT2))
        g = jnp.zeros_like(s)
        for h in range(PEER_HEADS):
            a_row = sc_ref[pl.ds(h * 2 * N_KEYS + ig, 1), :]
            ea_row = ea_ref[pl.ds(h * N_KEYS + ig, 1), :]
            b_t = sc_ref[h * 2 * N_KEYS + N_KEYS:(h + 1) * 2 * N_KEYS, :]
            eb_t = eb_ref[h * N_KEYS:(h + 1) * N_KEYS, :]
            sel = (a_row + b_t) >= tau_ref[h:h + 1, :]
            g = g + jnp.where(sel, ea_row * eb_t, 0.0)
        slabs.append((act * g).astype(BF16))
    a_t = jnp.concatenate(slabs, axis=0)
    o_ref[...] += jnp.dot(wu_ref[...], a_t, preferred_element_type=F32)


def _peer_mixer(h_t, wd, wu_t, sc_t, ea_t, eb_t, tau, *, tn, te):
    d, t = h_t.shape
    n_exp = wd.shape[0]
    return pl.pallas_call(
        _peer_kernel,
        grid=(t // tn, n_exp // te),
        in_specs=[
            pl.BlockSpec((d, tn), lambda i, j: (0, i)),
            pl.BlockSpec((te, d), lambda i, j: (j, 0)),
            pl.BlockSpec((d, te), lambda i, j: (0, j)),
            pl.BlockSpec((sc_t.shape[0], tn), lambda i, j: (0, i)),
            pl.BlockSpec((ea_t.shape[0], tn), lambda i, j: (0, i)),
            pl.BlockSpec((eb_t.shape[0], tn), lambda i, j: (0, i)),
            pl.BlockSpec((PEER_HEADS, tn), lambda i, j: (0, i)),
        ],
        out_specs=pl.BlockSpec((d, tn), lambda i, j: (0, i)),
        out_shape=jax.ShapeDtypeStruct((d, t), F32),
        compiler_params=_params("parallel", "arbitrary"),
        name="peer_mixer",
    )(h_t, wd, wu_t, sc_t, ea_t, eb_t, tau)


def _final_kernel(x_ref, p_ref, g_ref, o_ref):
    x = x_ref[...] + p_ref[...].T
    ms = jnp.mean(x * x, axis=-1, keepdims=True)
    o_ref[...] = x * lax.rsqrt(ms + EPS) * g_ref[...]


def _final(x1, peer_t, g, *, tm):
    t, d = x1.shape
    return pl.pallas_call(
        _final_kernel,
        grid=(t // tm,),
        in_specs=[pl.BlockSpec((tm, d), lambda i: (i, 0)), pl.BlockSpec((d, tm), lambda i: (0, i)),
                  pl.BlockSpec((1, d), lambda i: (0, 0))],
        out_specs=pl.BlockSpec((tm, d), lambda i: (i, 0)),
        out_shape=jax.ShapeDtypeStruct((t, d), F32),
        compiler_params=_params("parallel"),
        name="final_norm",
    )(x1, peer_t, g.reshape(1, d))


def _tiles(t, seq):
    return dict(
        norm_tm=min(256, t),
        mm_tm=min(1024, t),
        mm_tn=1024,
        qk_tm=min(256, seq),
        attn_tq=min(512, seq),
        attn_tk=min(512, seq),
        mix_tm=min(256, seq),
        score_tn=min(1024, t),
        select_tn=min(256, t),
        peer_tn=min(256, t),
        peer_te=512,
    )


def kernel(x, norm_mix_g, w_in, q_norm_g, k_norm_g, conv_w, attn_out_g, conv_out_g, w_out, norm_ffn_g,
           peer_w_query, peer_sub_keys, peer_w_down, peer_w_up, norm_final_g):
    batch, seq, d = x.shape
    depth = w_in.shape[0]
    t = batch * seq
    tl = _tiles(t, seq)
    xf = x.reshape(t, d)
    for layer in range(depth):
        wi = w_in[layer]
        q_end, k_end = ATTN_WIDTH, ATTN_WIDTH + KV_WIDTH
        v_end = k_end + KV_WIDTH
        w_in_b = jnp.concatenate([wi[:, :q_end], wi[:, v_end:], wi[:, q_end:k_end], wi[:, k_end:v_end]],
                                 axis=1).astype(BF16)
        w_out_b = w_out[layer].astype(BF16)
        wq_t = peer_w_query[layer].T.astype(BF16)
        wd_b = peer_w_down[layer].astype(BF16)
        wu_t = peer_w_up[layer].T.astype(BF16)

        h = _rmsnorm(xf, norm_mix_g[layer], tm=tl["norm_tm"])
        proj = _matmul(h, w_in_b, tm=tl["mm_tm"], tn=tl["mm_tn"], out_dtype=BF16, name="in_proj")
        q, k = _qk_prep(proj, q_norm_g[layer], k_norm_g[layer], seq, tm=tl["qk_tm"])
        attn = _attention(q, k, proj, batch, seq, tq=tl["attn_tq"], tk=tl["attn_tk"])
        y = _mix_norm(attn, proj, conv_w[layer], attn_out_g[layer], conv_out_g[layer], seq, tm=tl["mix_tm"])
        x1 = _matmul(y, w_out_b, tm=tl["mm_tm"] // 2, tn=tl["mm_tn"], out_dtype=F32, residual=xf, name="out_proj")

        h2_t = _rmsnorm(x1, norm_ffn_g[layer], tm=tl["norm_tm"], transpose=True)
        sc_t = _peer_scores(wq_t, h2_t, peer_sub_keys[layer], tn=tl["score_tn"])
        ea_t, eb_t, tau = _peer_select(sc_t, tn=tl["select_tn"])
        peer_t = _peer_mixer(h2_t, wd_b, wu_t, sc_t, ea_t, eb_t, tau, tn=tl["peer_tn"], te=tl["peer_te"])
        if layer == depth - 1:
            xf = _final(x1, peer_t, norm_final_g, tm=tl["norm_tm"])
        else:
            xf = x1 + peer_t.T
    return xf.reshape(batch, seq, d)
```

```python
import functools

import numpy as np
import jax
import jax.numpy as jnp
from jax import lax
from jax.experimental import pallas as pl
from jax.experimental.pallas import tpu as pltpu

F32 = jnp.float32
BF16 = jnp.bfloat16

HEAD_DIM = 128
N_Q_HEADS = 16
N_KV_HEADS = 4
GQA_GROUP = N_Q_HEADS // N_KV_HEADS
ATTN_WIDTH = N_Q_HEADS * HEAD_DIM
KV_WIDTH = N_KV_HEADS * HEAD_DIM
CONV_WIDTH = 2048
GRID_W = 64
ROPE_THETA = 10000.0
N_KEYS = 128
PEER_HEADS = 8
PEER_TOPK = 16
EPS = 1e-6
INV_SQRT2 = 0.7071067811865476

LANES = 128
BF16_SUBLANES = 16
VMEM_LIMIT_BYTES = 56 * 1024 * 1024

COL_Q = 0
COL_CX = ATTN_WIDTH
COL_GB = COL_CX + CONV_WIDTH
COL_GC = COL_GB + CONV_WIDTH
COL_K = COL_GC + CONV_WIDTH
COL_V = COL_K + KV_WIDTH
IN_COLS = COL_V + KV_WIDTH


def _params(*sem):
    return pltpu.CompilerParams(dimension_semantics=sem, vmem_limit_bytes=VMEM_LIMIT_BYTES)


def _rmsnorm_kernel(x_ref, g_ref, o_ref, *, transpose):
    x = x_ref[...]
    ms = jnp.mean(x * x, axis=-1, keepdims=True)
    y = x * lax.rsqrt(ms + EPS) * g_ref[...]
    if transpose:
        y = y.T
    o_ref[...] = y.astype(o_ref.dtype)


def _rmsnorm(x, g, *, tm, transpose=False):
    t, d = x.shape
    if transpose:
        out_shape = jax.ShapeDtypeStruct((d, t), BF16)
        out_spec = pl.BlockSpec((d, tm), lambda i: (0, i))
    else:
        out_shape = jax.ShapeDtypeStruct((t, d), BF16)
        out_spec = pl.BlockSpec((tm, d), lambda i: (i, 0))
    return pl.pallas_call(
        functools.partial(_rmsnorm_kernel, transpose=transpose),
        grid=(t // tm,),
        in_specs=[pl.BlockSpec((tm, d), lambda i: (i, 0)), pl.BlockSpec((1, d), lambda i: (0, 0))],
        out_specs=out_spec,
        out_shape=out_shape,
        compiler_params=_params("parallel"),
        name="rmsnorm_t" if transpose else "rmsnorm",
    )(x, g.reshape(1, d))


def _mm_kernel(a_ref, b_ref, o_ref):
    o_ref[...] = jnp.dot(a_ref[...], b_ref[...], preferred_element_type=F32).astype(o_ref.dtype)


def _mm_res_kernel(a_ref, b_ref, r_ref, o_ref):
    acc = jnp.dot(a_ref[...], b_ref[...], preferred_element_type=F32)
    o_ref[...] = (r_ref[...] + acc).astype(o_ref.dtype)


def _matmul(a, b, *, tm, tn, out_dtype, residual=None, name):
    m, k = a.shape
    _, n = b.shape
    in_specs = [pl.BlockSpec((tm, k), lambda i, j: (i, 0)), pl.BlockSpec((k, tn), lambda i, j: (0, j))]
    args = [a, b]
    body = _mm_kernel
    if residual is not None:
        in_specs.append(pl.BlockSpec((tm, tn), lambda i, j: (i, j)))
        args.append(residual)
        body = _mm_res_kernel
    return pl.pallas_call(
        body,
        grid=(m // tm, n // tn),
        in_specs=in_specs,
        out_specs=pl.BlockSpec((tm, tn), lambda i, j: (i, j)),
        out_shape=jax.ShapeDtypeStruct((m, n), out_dtype),
        compiler_params=_params("parallel", "parallel"),
        name=name,
    )(*args)


def _rope_tables(seq):
    t = np.arange(seq)
    half = HEAD_DIM // 2
    inv = ROPE_THETA ** (-np.arange(0, half, 2, dtype=np.float64) / half)
    ang_row = (t // GRID_W)[:, None] * inv
    ang_col = (t % GRID_W)[:, None] * inv
    zeros = np.zeros_like(ang_row)
    cos = np.concatenate([np.cos(ang_row)] * 2 + [np.cos(ang_col)] * 2, axis=1)
    sa = np.concatenate([-np.sin(ang_row), zeros, -np.sin(ang_col), zeros], axis=1)
    sb = np.concatenate([zeros, np.sin(ang_row), zeros, np.sin(ang_col)], axis=1)
    return (jnp.asarray(cos, F32), jnp.asarray(sa, F32), jnp.asarray(sb, F32))


def _qk_kernel(q_ref, k_ref, c_ref, sa_ref, sb_ref, gq_ref, gk_ref, qo_ref, ko_ref):
    c, sa, sb = c_ref[...], sa_ref[...], sb_ref[...]
    quarter = HEAD_DIM // 4

    def prep(x, g, scale):
        ms = jnp.mean(x * x, axis=-1, keepdims=True)
        y = x * lax.rsqrt(ms + EPS) * g
        r = y * c + pltpu.roll(y, HEAD_DIM - quarter, 1) * sa + pltpu.roll(y, quarter, 1) * sb
        return r * scale

    for h in range(N_Q_HEADS):
        sl = slice(h * HEAD_DIM, (h + 1) * HEAD_DIM)
        qo_ref[:, sl] = prep(q_ref[:, sl].astype(F32), gq_ref[...], HEAD_DIM ** -0.5).astype(qo_ref.dtype)
    for h in range(N_KV_HEADS):
        sl = slice(h * HEAD_DIM, (h + 1) * HEAD_DIM)
        ko_ref[:, sl] = prep(k_ref[:, sl].astype(F32), gk_ref[...], 1.0).astype(ko_ref.dtype)


def _qk_prep(proj, gq, gk, seq, *, tm):
    t = proj.shape[0]
    tables = _rope_tables(seq)
    per_seq = seq // tm
    tab_spec = pl.BlockSpec((tm, HEAD_DIM), lambda i: (i % per_seq, 0))
    g_spec = pl.BlockSpec((1, HEAD_DIM), lambda i: (0, 0))
    return pl.pallas_call(
        _qk_kernel,
        grid=(t // tm,),
        in_specs=[
            pl.BlockSpec((tm, ATTN_WIDTH), lambda i: (i, COL_Q // ATTN_WIDTH)),
            pl.BlockSpec((tm, KV_WIDTH), lambda i: (i, COL_K // KV_WIDTH)),
            tab_spec, tab_spec, tab_spec, g_spec, g_spec,
        ],
        out_specs=[pl.BlockSpec((tm, ATTN_WIDTH), lambda i: (i, 0)), pl.BlockSpec((tm, KV_WIDTH), lambda i: (i, 0))],
        out_shape=[jax.ShapeDtypeStruct((t, ATTN_WIDTH), BF16), jax.ShapeDtypeStruct((t, KV_WIDTH), BF16)],
        compiler_params=_params("parallel"),
        name="qk_prep",
    )(proj, proj, *tables, gq.reshape(1, HEAD_DIM), gk.reshape(1, HEAD_DIM))


def _attn_kernel(q_ref, k_ref, v_ref, o_ref, *, tk):
    tq = q_ref.shape[0]
    n_chunks = k_ref.shape[0] // tk
    for h in range(GQA_GROUP):
        sl = slice(h * HEAD_DIM, (h + 1) * HEAD_DIM)
        q = q_ref[:, sl]

        def body(i, carry, q=q):
            m, l, acc = carry
            off = pl.multiple_of(i * tk, tk)
            kc = k_ref[pl.ds(off, tk), :]
            vc = v_ref[pl.ds(off, tk), :]
            s = lax.dot_general(q, kc, (((1,), (1,)), ((), ())), preferred_element_type=F32)
            m_new = jnp.maximum(m, jnp.max(s, axis=-1, keepdims=True))
            alpha = jnp.exp(m - m_new)
            p = jnp.exp(s - m_new)
            l = alpha * l + jnp.sum(p, axis=-1, keepdims=True)
            acc = alpha * acc + jnp.dot(p.astype(BF16), vc, preferred_element_type=F32)
            return m_new, l, acc

        init = (jnp.full((tq, 1), -jnp.inf, F32), jnp.zeros((tq, 1), F32), jnp.zeros((tq, HEAD_DIM), F32))
        _, l, acc = lax.fori_loop(0, n_chunks, body, init)
        o_ref[:, sl] = (acc / l).astype(o_ref.dtype)


def _attention(q, k, proj, batch, seq, *, tq, tk):
    t = q.shape[0]
    nq = seq // tq
    group_w = GQA_GROUP * HEAD_DIM
    return pl.pallas_call(
        functools.partial(_attn_kernel, tk=tk),
        grid=(batch, N_KV_HEADS, nq),
        in_specs=[
            pl.BlockSpec((tq, group_w), lambda b, g, i: (b * nq + i, g)),
            pl.BlockSpec((seq, HEAD_DIM), lambda b, g, i: (b, g)),
            pl.BlockSpec((seq, HEAD_DIM), lambda b, g, i: (b, COL_V // HEAD_DIM + g)),
        ],
        out_specs=pl.BlockSpec((tq, group_w), lambda b, g, i: (b * nq + i, g)),
        out_shape=jax.ShapeDtypeStruct((t, ATTN_WIDTH), BF16),
        compiler_params=_params("parallel", "parallel", "parallel"),
        name="attention",
    )(q, k, proj)


def _mix_kernel(attn_ref, cx_ref, gb_ref, gc_ref, cxp_ref, gcp_ref, cxn_ref, gcn_ref, w_ref, ga_ref, gv_ref,
                o_ref, *, per_seq):
    i = pl.program_id(0)
    tm = cx_ref.shape[0]
    u = gc_ref[...].astype(F32) * cx_ref[...].astype(F32)
    last_row = BF16_SUBLANES - 1
    prev = gcp_ref[last_row:, :].astype(F32) * cxp_ref[last_row:, :].astype(F32)
    nxt = gcn_ref[0:1, :].astype(F32) * cxn_ref[0:1, :].astype(F32)
    pos = i % per_seq
    prev = jnp.where(pos == 0, 0.0, prev)
    nxt = jnp.where(pos == per_seq - 1, 0.0, nxt)
    row = lax.broadcasted_iota(jnp.int32, (tm, 1), 0)
    u_dn = jnp.where(row == 0, prev, pltpu.roll(u, 1, 0))
    u_up = jnp.where(row == tm - 1, nxt, pltpu.roll(u, tm - 1, 0))
    w = w_ref[...]
    conv = gb_ref[...].astype(F32) * (u_dn * w[0:1] + u * w[1:2] + u_up * w[2:3])

    def norm(z, g):
        ms = jnp.mean(z * z, axis=-1, keepdims=True)
        return z * lax.rsqrt(ms + EPS) * g

    o_ref[:, :ATTN_WIDTH] = norm(attn_ref[...].astype(F32), ga_ref[...]).astype(o_ref.dtype)
    o_ref[:, ATTN_WIDTH:] = norm(conv, gv_ref[...]).astype(o_ref.dtype)


def _mix_norm(attn, proj, conv_w, ga, gv, seq, *, tm):
    t = attn.shape[0]
    per_seq = seq // tm
    halo = BF16_SUBLANES
    n_halo = t // halo
    cw = CONV_WIDTH

    def main(col):
        return pl.BlockSpec((tm, cw), lambda i: (i, col // cw))

    def prev(col):
        return pl.BlockSpec((halo, cw), lambda i: (jnp.maximum(i * (tm // halo) - 1, 0), col // cw))

    def nxt(col):
        return pl.BlockSpec((halo, cw), lambda i: (jnp.minimum((i + 1) * (tm // halo), n_halo - 1), col // cw))

    def vec(n):
        return pl.BlockSpec((n, cw), lambda i: (0, 0))

    return pl.pallas_call(
        functools.partial(_mix_kernel, per_seq=per_seq),
        grid=(t // tm,),
        in_specs=[pl.BlockSpec((tm, ATTN_WIDTH), lambda i: (i, 0)), main(COL_CX), main(COL_GB), main(COL_GC),
                  prev(COL_CX), prev(COL_GC), nxt(COL_CX), nxt(COL_GC), vec(3), vec(1), vec(1)],
        out_specs=pl.BlockSpec((tm, ATTN_WIDTH + cw), lambda i: (i, 0)),
        out_shape=jax.ShapeDtypeStruct((t, ATTN_WIDTH + cw), BF16),
        compiler_params=_params("parallel"),
        name="mix_norm",
    )(attn, proj, proj, proj, proj, proj, proj, proj, conv_w, ga.reshape(1, -1), gv.reshape(1, -1))


def _peer_scores_kernel(wq_ref, h_ref, sk_ref, a_ref, b_ref):
    qt = jnp.dot(wq_ref[...], h_ref[...], preferred_element_type=F32)
    for p, o_ref in enumerate((a_ref, b_ref)):
        rows = slice(p * N_KEYS, (p + 1) * N_KEYS)
        o_ref[...] = jnp.dot(sk_ref[0, p], qt[rows, :], preferred_element_type=F32,
                             precision=lax.Precision.HIGHEST)


def _peer_scores(wq_t, h_t, sub_keys, *, tn):
    d = wq_t.shape[1]
    t = h_t.shape[1]
    out = jax.ShapeDtypeStruct((PEER_HEADS * N_KEYS, t), F32)
    return pl.pallas_call(
        _peer_scores_kernel,
        grid=(t // tn, PEER_HEADS),
        in_specs=[
            pl.BlockSpec((2 * N_KEYS, d), lambda i, h: (h, 0)),
            pl.BlockSpec((d, tn), lambda i, h: (0, i)),
            pl.BlockSpec((1, 2, N_KEYS, N_KEYS), lambda i, h: (h, 0, 0, 0)),
        ],
        out_specs=[pl.BlockSpec((N_KEYS, tn), lambda i, h: (h, i))] * 2,
        out_shape=[out, out],
        compiler_params=_params("parallel", "parallel"),
        name="peer_scores",
    )(wq_t, h_t, sub_keys)


def _top_sorted(work, n):
    out = []
    for _ in range(n):
        m = jnp.max(work, axis=0, keepdims=True)
        out.append(m)
        work = jnp.where(work == m, -jnp.inf, work)
    return out


SELECT_TOP_ROWS = 24


def _peer_select_kernel(a_ref, b_ref, thr_ref, ea_ref, eb_ref, a_top, b_top):
    k = PEER_TOPK
    sub = 8
    a_top[...] = jnp.full(a_top.shape, -jnp.inf, F32)
    b_top[...] = jnp.full(b_top.shape, -jnp.inf, F32)
    for h in range(PEER_HEADS):
        rows = slice(h * N_KEYS, (h + 1) * N_KEYS)
        a = a_ref[rows, :]
        b = b_ref[rows, :]
        a_sorted = _top_sorted(a, k + 1)
        b_sorted = _top_sorted(b, k + 1)
        for r in range(k + 1):
            a_top[r:r + 1, :] = a_sorted[r]
            b_top[r:r + 1, :] = b_sorted[r]
        slabs = [a_sorted[0] + b_top[...]]
        for r in range(1, sub):
            slabs.append(a_sorted[r] + b_top[0:sub, :])
        slabs.append(a_top[sub:, :] + b_sorted[0])
        top = _top_sorted(jnp.concatenate(slabs, axis=0), k + 1)
        z = jnp.ones_like(top[0])
        for r in range(1, k):
            z = z + jnp.exp(top[r] - top[0])
        tau = 0.5 * (top[k - 1] + top[k])
        thr_ref[:, h, :] = tau - a
        ea_ref[:, h, :] = jnp.exp(a - a_sorted[0])
        eb_ref[rows, :] = jnp.exp(b - b_sorted[0]) / z


def _peer_select(a_t, b_t, *, tn):
    rows, t = a_t.shape
    spec = pl.BlockSpec((rows, tn), lambda i: (0, i))
    out = jax.ShapeDtypeStruct((rows, t), F32)
    key_spec = pl.BlockSpec((N_KEYS, PEER_HEADS, tn), lambda i: (0, 0, i))
    key_out = jax.ShapeDtypeStruct((N_KEYS, PEER_HEADS, t), F32)
    return pl.pallas_call(
        _peer_select_kernel,
        grid=(t // tn,),
        in_specs=[spec, spec],
        out_specs=[key_spec, key_spec, spec],
        out_shape=[key_out, key_out, out],
        scratch_shapes=[pltpu.VMEM((SELECT_TOP_ROWS, tn), F32), pltpu.VMEM((SELECT_TOP_ROWS, tn), F32)],
        compiler_params=_params("parallel"),
        name="peer_select",
    )(a_t, b_t)


def _peer_kernel(h_ref, wd_ref, wu_ref, thr_ref, ea_ref, b_ref, eb_ref, o_ref, s_buf, a_buf, *, n_blocks, n_exp_blocks):
    s = pl.program_id(0)
    n_slabs = wd_ref.shape[0] // N_KEYS

    @pl.when(s == 0)
    def _():
        s_buf[...] = jnp.zeros_like(s_buf)
        a_buf[...] = jnp.zeros_like(a_buf)
        o_ref[...] = jnp.zeros_like(o_ref)

    def step(cur, prv):
        first_key = (jnp.clip(s - 1, 0, n_blocks - 1) % n_exp_blocks) * n_slabs
        first = (jnp.maximum(s - 2, 0) % n_exp_blocks) == 0
        out_rows = o_ref.shape[0] // n_slabs
        for ii in range(n_slabs):
            rows = slice(ii * N_KEYS, (ii + 1) * N_KEYS)
            if ii % 2 == 0:
                pair = slice(ii * N_KEYS, (ii + 2) * N_KEYS)
                s_buf[cur, pair, :] = jnp.dot(wd_ref[pair, :], h_ref[...], preferred_element_type=F32)

            key = first_key + ii
            for c in range(s_buf.shape[2] // LANES):
                cols = slice(c * LANES, (c + 1) * LANES)
                g = jnp.zeros((N_KEYS, LANES), F32)
                for h in range(PEER_HEADS):
                    head = slice(h * N_KEYS, (h + 1) * N_KEYS)
                    thr_row = thr_ref[key, pl.ds(h, 1), cols]
                    ea_row = ea_ref[key, pl.ds(h, 1), cols]
                    g = g + jnp.where(b_ref[head, cols] >= thr_row, eb_ref[head, cols], 0.0) * ea_row
                sv = s_buf[prv, rows, cols]
                act = 0.5 * sv * (1.0 + lax.erf(sv * INV_SQRT2))
                a_buf[prv, rows, cols] = (act * g).astype(a_buf.dtype)

            orow = slice(ii * out_rows, (ii + 1) * out_rows)
            upd = jnp.dot(wu_ref[orow, :], a_buf[cur], preferred_element_type=F32)
            o_ref[orow, :] = jnp.where(first, upd, o_ref[orow, :] + upd)

    for parity in range(2):
        pl.when(s % 2 == parity)(functools.partial(step, parity, 1 - parity))


def _peer_mixer(h_t, wd, wu_t, thr_t, ea_t, b_t, eb_t, *, tn, te):
    d, t = h_t.shape
    n_exp_blocks = wd.shape[0] // te
    n_blocks = (t // tn) * n_exp_blocks
    rows = b_t.shape[0]

    def blk(s, lag):
        return jnp.clip(s - lag, 0, n_blocks - 1)

    once = pl.Buffered(1)
    tok_spec = pl.BlockSpec((rows, tn), lambda s: (0, blk(s, 1) // n_exp_blocks), pipeline_mode=once)
    key_spec = pl.BlockSpec((N_KEYS, PEER_HEADS, tn), lambda s: (0, 0, blk(s, 1) // n_exp_blocks), pipeline_mode=once)
    return pl.pallas_call(
        functools.partial(_peer_kernel, n_blocks=n_blocks, n_exp_blocks=n_exp_blocks),
        grid=(n_blocks + 2,),
        in_specs=[
            pl.BlockSpec((d, tn), lambda s: (0, blk(s, 0) // n_exp_blocks), pipeline_mode=once),
            pl.BlockSpec((te, d), lambda s: (blk(s, 0) % n_exp_blocks, 0)),
            pl.BlockSpec((d, te), lambda s: (0, blk(s, 2) % n_exp_blocks)),
            key_spec, key_spec, tok_spec, tok_spec,
        ],
        out_specs=pl.BlockSpec((d, tn), lambda s: (0, blk(s, 2) // n_exp_blocks)),
        out_shape=jax.ShapeDtypeStruct((d, t), F32),
        scratch_shapes=[pltpu.VMEM((2, te, tn), F32), pltpu.VMEM((2, te, tn), BF16)],
        compiler_params=_params("arbitrary"),
        name="peer_mixer",
    )(h_t, wd, wu_t, thr_t, ea_t, b_t, eb_t)


def _final_kernel(x_ref, p_ref, g_ref, o_ref):
    x = x_ref[...] + p_ref[...].T
    ms = jnp.mean(x * x, axis=-1, keepdims=True)
    o_ref[...] = x * lax.rsqrt(ms + EPS) * g_ref[...]


def _final(x1, peer_t, g, *, tm):
    t, d = x1.shape
    return pl.pallas_call(
        _final_kernel,
        grid=(t // tm,),
        in_specs=[pl.BlockSpec((tm, d), lambda i: (i, 0)), pl.BlockSpec((d, tm), lambda i: (0, i)),
                  pl.BlockSpec((1, d), lambda i: (0, 0))],
        out_specs=pl.BlockSpec((tm, d), lambda i: (i, 0)),
        out_shape=jax.ShapeDtypeStruct((t, d), F32),
        compiler_params=_params("parallel"),
        name="final_norm",
    )(x1, peer_t, g.reshape(1, d))


def _tiles(t, seq):
    return dict(
        norm_tm=min(256, t),
        mm_tm=min(1024, t),
        mm_tn=1024,
        qk_tm=min(256, seq),
        attn_tq=min(512, seq),
        attn_tk=min(512, seq),
        mix_tm=min(256, seq),
        score_tn=min(1024, t),
        select_tn=min(256, t),
        peer_tn=min(512, t),
        peer_te=512,
    )


def kernel(x, norm_mix_g, w_in, q_norm_g, k_norm_g, conv_w, attn_out_g, conv_out_g, w_out, norm_ffn_g,
           peer_w_query, peer_sub_keys, peer_w_down, peer_w_up, norm_final_g):
    batch, seq, d = x.shape
    depth = w_in.shape[0]
    t = batch * seq
    tl = _tiles(t, seq)
    xf = x.reshape(t, d)
    for layer in range(depth):
        wi = w_in[layer]
        q_end, k_end = ATTN_WIDTH, ATTN_WIDTH + KV_WIDTH
        v_end = k_end + KV_WIDTH
        w_in_b = jnp.concatenate([wi[:, :q_end], wi[:, v_end:], wi[:, q_end:k_end], wi[:, k_end:v_end]],
                                 axis=1).astype(BF16)
        w_out_b = w_out[layer].astype(BF16)
        wq_t = peer_w_query[layer].T.astype(BF16)
        wd_b = peer_w_down[layer].astype(BF16)
        wu_t = peer_w_up[layer].T.astype(BF16)

        h = _rmsnorm(xf, norm_mix_g[layer], tm=tl["norm_tm"])
        proj = _matmul(h, w_in_b, tm=tl["mm_tm"], tn=tl["mm_tn"], out_dtype=BF16, name="in_proj")
        q, k = _qk_prep(proj, q_norm_g[layer], k_norm_g[layer], seq, tm=tl["qk_tm"])
        attn = _attention(q, k, proj, batch, seq, tq=tl["attn_tq"], tk=tl["attn_tk"])
        y = _mix_norm(attn, proj, conv_w[layer], attn_out_g[layer], conv_out_g[layer], seq, tm=tl["mix_tm"])
        x1 = _matmul(y, w_out_b, tm=tl["mm_tm"] // 2, tn=tl["mm_tn"], out_dtype=F32, residual=xf, name="out_proj")

        h2_t = _rmsnorm(x1, norm_ffn_g[layer], tm=tl["norm_tm"], transpose=True)
        a_t, b_t = _peer_scores(wq_t, h2_t, peer_sub_keys[layer], tn=tl["score_tn"])
        thr_t, ea_t, eb_t = _peer_select(a_t, b_t, tn=tl["select_tn"])
        peer_t = _peer_mixer(h2_t, wd_b, wu_t, thr_t, ea_t, b_t, eb_t, tn=tl["peer_tn"], te=tl["peer_te"])
        if layer == depth - 1:
            xf = _final(x1, peer_t, norm_final_g, tm=tl["norm_tm"])
        else:
            xf = x1 + peer_t.T
    return xf.reshape(batch, seq, d)
```

```python
import functools

import numpy as np
import jax
import jax.numpy as jnp
from jax import lax
from jax.experimental import pallas as pl
from jax.experimental.pallas import tpu as pltpu

F32 = jnp.float32
BF16 = jnp.bfloat16

HEAD_DIM = 128
N_Q_HEADS = 16
N_KV_HEADS = 4
GQA_GROUP = N_Q_HEADS // N_KV_HEADS
ATTN_WIDTH = N_Q_HEADS * HEAD_DIM
KV_WIDTH = N_KV_HEADS * HEAD_DIM
CONV_WIDTH = 2048
GRID_W = 64
ROPE_THETA = 10000.0
N_KEYS = 128
PEER_HEADS = 8
PEER_TOPK = 16
EPS = 1e-6
INV_SQRT2 = 0.7071067811865476
LOG2_E = 1.4426950408889634
Q_SCALE = HEAD_DIM ** -0.5 * LOG2_E

LANES = 128
BF16_SUBLANES = 16
VMEM_LIMIT_BYTES = 56 * 1024 * 1024

COL_Q = 0
COL_CX = ATTN_WIDTH
COL_GB = COL_CX + CONV_WIDTH
COL_GC = COL_GB + CONV_WIDTH
COL_K = COL_GC + CONV_WIDTH
COL_V = COL_K + KV_WIDTH
IN_COLS = COL_V + KV_WIDTH


def _params(*sem):
    return pltpu.CompilerParams(dimension_semantics=sem, vmem_limit_bytes=VMEM_LIMIT_BYTES)


def _rmsnorm_kernel(x_ref, g_ref, o_ref, *, transpose):
    x = x_ref[...]
    ms = jnp.mean(x * x, axis=-1, keepdims=True)
    y = x * lax.rsqrt(ms + EPS) * g_ref[...]
    if transpose:
        y = y.T
    o_ref[...] = y.astype(o_ref.dtype)


def _rmsnorm(x, g, *, tm, transpose=False):
    t, d = x.shape
    if transpose:
        out_shape = jax.ShapeDtypeStruct((d, t), BF16)
        out_spec = pl.BlockSpec((d, tm), lambda i: (0, i))
    else:
        out_shape = jax.ShapeDtypeStruct((t, d), BF16)
        out_spec = pl.BlockSpec((tm, d), lambda i: (i, 0))
    return pl.pallas_call(
        functools.partial(_rmsnorm_kernel, transpose=transpose),
        grid=(t // tm,),
        in_specs=[pl.BlockSpec((tm, d), lambda i: (i, 0)), pl.BlockSpec((1, d), lambda i: (0, 0))],
        out_specs=out_spec,
        out_shape=out_shape,
        compiler_params=_params("parallel"),
        name="rmsnorm_t" if transpose else "rmsnorm",
    )(x, g.reshape(1, d))


def _mm_kernel(a_ref, b_ref, o_ref):
    o_ref[...] = jnp.dot(a_ref[...], b_ref[...], preferred_element_type=F32).astype(o_ref.dtype)


def _mm_res_kernel(a_ref, b_ref, r_ref, o_ref):
    acc = jnp.dot(a_ref[...], b_ref[...], preferred_element_type=F32)
    o_ref[...] = (r_ref[...] + acc).astype(o_ref.dtype)


def _matmul(a, b, *, tm, tn, out_dtype, residual=None, name):
    m, k = a.shape
    _, n = b.shape
    in_specs = [pl.BlockSpec((tm, k), lambda i, j: (i, 0)), pl.BlockSpec((k, tn), lambda i, j: (0, j))]
    args = [a, b]
    body = _mm_kernel
    if residual is not None:
        in_specs.append(pl.BlockSpec((tm, tn), lambda i, j: (i, j)))
        args.append(residual)
        body = _mm_res_kernel
    return pl.pallas_call(
        body,
        grid=(m // tm, n // tn),
        in_specs=in_specs,
        out_specs=pl.BlockSpec((tm, tn), lambda i, j: (i, j)),
        out_shape=jax.ShapeDtypeStruct((m, n), out_dtype),
        compiler_params=_params("parallel", "parallel"),
        name=name,
    )(*args)


def _rope_tables(seq):
    t = np.arange(seq)
    half = HEAD_DIM // 2
    inv = ROPE_THETA ** (-np.arange(0, half, 2, dtype=np.float64) / half)
    ang_row = (t // GRID_W)[:, None] * inv
    ang_col = (t % GRID_W)[:, None] * inv
    zeros = np.zeros_like(ang_row)
    cos = np.concatenate([np.cos(ang_row)] * 2 + [np.cos(ang_col)] * 2, axis=1)
    sa = np.concatenate([-np.sin(ang_row), zeros, -np.sin(ang_col), zeros], axis=1)
    sb = np.concatenate([zeros, np.sin(ang_row), zeros, np.sin(ang_col)], axis=1)
    return (jnp.asarray(cos, F32), jnp.asarray(sa, F32), jnp.asarray(sb, F32))


def _qk_kernel(q_ref, k_ref, c_ref, sa_ref, sb_ref, gq_ref, gk_ref, qo_ref, ko_ref):
    c, sa, sb = c_ref[...], sa_ref[...], sb_ref[...]
    quarter = HEAD_DIM // 4

    def prep(x, g, scale):
        ms = jnp.mean(x * x, axis=-1, keepdims=True)
        y = x * lax.rsqrt(ms + EPS) * g
        r = y * c + pltpu.roll(y, HEAD_DIM - quarter, 1) * sa + pltpu.roll(y, quarter, 1) * sb
        return r * scale

    for h in range(N_Q_HEADS):
        sl = slice(h * HEAD_DIM, (h + 1) * HEAD_DIM)
        qo_ref[:, sl] = prep(q_ref[:, sl].astype(F32), gq_ref[...], Q_SCALE).astype(qo_ref.dtype)
    for h in range(N_KV_HEADS):
        sl = slice(h * HEAD_DIM, (h + 1) * HEAD_DIM)
        ko_ref[:, sl] = prep(k_ref[:, sl].astype(F32), gk_ref[...], 1.0).astype(ko_ref.dtype)


def _qk_prep(proj, gq, gk, seq, *, tm):
    t = proj.shape[0]
    tables = _rope_tables(seq)
    per_seq = seq // tm
    tab_spec = pl.BlockSpec((tm, HEAD_DIM), lambda i: (i % per_seq, 0))
    g_spec = pl.BlockSpec((1, HEAD_DIM), lambda i: (0, 0))
    return pl.pallas_call(
        _qk_kernel,
        grid=(t // tm,),
        in_specs=[
            pl.BlockSpec((tm, ATTN_WIDTH), lambda i: (i, COL_Q // ATTN_WIDTH)),
            pl.BlockSpec((tm, KV_WIDTH), lambda i: (i, COL_K // KV_WIDTH)),
            tab_spec, tab_spec, tab_spec, g_spec, g_spec,
        ],
        out_specs=[pl.BlockSpec((tm, ATTN_WIDTH), lambda i: (i, 0)), pl.BlockSpec((tm, KV_WIDTH), lambda i: (i, 0))],
        out_shape=[jax.ShapeDtypeStruct((t, ATTN_WIDTH), BF16), jax.ShapeDtypeStruct((t, KV_WIDTH), BF16)],
        compiler_params=_params("parallel"),
        name="qk_prep",
    )(proj, proj, *tables, gq.reshape(1, HEAD_DIM), gk.reshape(1, HEAD_DIM))


def _attn_kernel(q_ref, k_ref, v_ref, o_ref, *, tk):
    tq = q_ref.shape[0]
    n_chunks = k_ref.shape[0] // tk
    ones = jnp.ones((tk, HEAD_DIM), v_ref.dtype)
    for h in range(GQA_GROUP):
        sl = slice(h * HEAD_DIM, (h + 1) * HEAD_DIM)
        q = q_ref[:, sl]

        def body(i, carry, q=q):
            m, acc = carry
            off = pl.multiple_of(i * tk, tk)
            kc = k_ref[pl.ds(off, tk), :]
            vc = jnp.concatenate([v_ref[pl.ds(off, tk), :], ones], axis=1)
            s = lax.dot_general(q, kc, (((1,), (1,)), ((), ())), preferred_element_type=F32)
            m_new = jnp.maximum(m, jnp.max(s, axis=-1, keepdims=True))
            alpha = jnp.exp2(m - m_new)
            p = jnp.exp2(s - m_new)
            acc = alpha * acc + jnp.dot(p.astype(BF16), vc, preferred_element_type=F32)
            return m_new, acc

        init = (jnp.full((tq, 1), -jnp.inf, F32), jnp.zeros((tq, 2 * HEAD_DIM), F32))
        _, acc = lax.fori_loop(0, n_chunks, body, init, unroll=True)
        o_ref[:, sl] = (acc[:, :HEAD_DIM] / acc[:, HEAD_DIM:]).astype(o_ref.dtype)


def _attention(q, k, proj, batch, seq, *, tq, tk):
    t = q.shape[0]
    nq = seq // tq
    group_w = GQA_GROUP * HEAD_DIM
    return pl.pallas_call(
        functools.partial(_attn_kernel, tk=tk),
        grid=(batch, N_KV_HEADS, nq),
        in_specs=[
            pl.BlockSpec((tq, group_w), lambda b, g, i: (b * nq + i, g)),
            pl.BlockSpec((seq, HEAD_DIM), lambda b, g, i: (b, g)),
            pl.BlockSpec((seq, HEAD_DIM), lambda b, g, i: (b, COL_V // HEAD_DIM + g)),
        ],
        out_specs=pl.BlockSpec((tq, group_w), lambda b, g, i: (b * nq + i, g)),
        out_shape=jax.ShapeDtypeStruct((t, ATTN_WIDTH), BF16),
        compiler_params=_params("parallel", "parallel", "parallel"),
        name="attention",
    )(q, k, proj)


def _mix_kernel(attn_ref, cx_ref, gb_ref, gc_ref, cxp_ref, gcp_ref, cxn_ref, gcn_ref, w_ref, ga_ref, gv_ref,
                o_ref, *, per_seq):
    i = pl.program_id(0)
    tm = cx_ref.shape[0]
    u = gc_ref[...].astype(F32) * cx_ref[...].astype(F32)
    last_row = BF16_SUBLANES - 1
    prev = gcp_ref[last_row:, :].astype(F32) * cxp_ref[last_row:, :].astype(F32)
    nxt = gcn_ref[0:1, :].astype(F32) * cxn_ref[0:1, :].astype(F32)
    pos = i % per_seq
    prev = jnp.where(pos == 0, 0.0, prev)
    nxt = jnp.where(pos == per_seq - 1, 0.0, nxt)
    row = lax.broadcasted_iota(jnp.int32, (tm, 1), 0)
    u_dn = jnp.where(row == 0, prev, pltpu.roll(u, 1, 0))
    u_up = jnp.where(row == tm - 1, nxt, pltpu.roll(u, tm - 1, 0))
    w = w_ref[...]
    conv = gb_ref[...].astype(F32) * (u_dn * w[0:1] + u * w[1:2] + u_up * w[2:3])

    def norm(z, g):
        ms = jnp.mean(z * z, axis=-1, keepdims=True)
        return z * lax.rsqrt(ms + EPS) * g

    o_ref[:, :ATTN_WIDTH] = norm(attn_ref[...].astype(F32), ga_ref[...]).astype(o_ref.dtype)
    o_ref[:, ATTN_WIDTH:] = norm(conv, gv_ref[...]).astype(o_ref.dtype)


def _mix_norm(attn, proj, conv_w, ga, gv, seq, *, tm):
    t = attn.shape[0]
    per_seq = seq // tm
    halo = BF16_SUBLANES
    n_halo = t // halo
    cw = CONV_WIDTH

    def main(col):
        return pl.BlockSpec((tm, cw), lambda i: (i, col // cw))

    def prev(col):
        return pl.BlockSpec((halo, cw), lambda i: (jnp.maximum(i * (tm // halo) - 1, 0), col // cw))

    def nxt(col):
        return pl.BlockSpec((halo, cw), lambda i: (jnp.minimum((i + 1) * (tm // halo), n_halo - 1), col // cw))

    def vec(n):
        return pl.BlockSpec((n, cw), lambda i: (0, 0))

    return pl.pallas_call(
        functools.partial(_mix_kernel, per_seq=per_seq),
        grid=(t // tm,),
        in_specs=[pl.BlockSpec((tm, ATTN_WIDTH), lambda i: (i, 0)), main(COL_CX), main(COL_GB), main(COL_GC),
                  prev(COL_CX), prev(COL_GC), nxt(COL_CX), nxt(COL_GC), vec(3), vec(1), vec(1)],
        out_specs=pl.BlockSpec((tm, ATTN_WIDTH + cw), lambda i: (i, 0)),
        out_shape=jax.ShapeDtypeStruct((t, ATTN_WIDTH + cw), BF16),
        compiler_params=_params("parallel"),
        name="mix_norm",
    )(attn, proj, proj, proj, proj, proj, proj, proj, conv_w, ga.reshape(1, -1), gv.reshape(1, -1))


def _peer_scores_kernel(wq_ref, h_ref, sk_ref, a_ref, b_ref):
    qt = jnp.dot(wq_ref[...], h_ref[...], preferred_element_type=F32)
    for p, o_ref in enumerate((a_ref, b_ref)):
        rows = slice(p * N_KEYS, (p + 1) * N_KEYS)
        o_ref[...] = jnp.dot(sk_ref[0, p], qt[rows, :], preferred_element_type=F32,
                             precision=lax.Precision.HIGHEST)


def _peer_scores(wq_t, h_t, sub_keys, *, tn):
    d = wq_t.shape[1]
    t = h_t.shape[1]
    out = jax.ShapeDtypeStruct((PEER_HEADS * N_KEYS, t), F32)
    return pl.pallas_call(
        _peer_scores_kernel,
        grid=(t // tn, PEER_HEADS),
        in_specs=[
            pl.BlockSpec((2 * N_KEYS, d), lambda i, h: (h, 0)),
            pl.BlockSpec((d, tn), lambda i, h: (0, i)),
            pl.BlockSpec((1, 2, N_KEYS, N_KEYS), lambda i, h: (h, 0, 0, 0)),
        ],
        out_specs=[pl.BlockSpec((N_KEYS, tn), lambda i, h: (h, i))] * 2,
        out_shape=[out, out],
        compiler_params=_params("parallel", "parallel"),
        name="peer_scores",
    )(wq_t, h_t, sub_keys)


def _top_sorted(work, n):
    out = []
    for _ in range(n):
        m = jnp.max(work, axis=0, keepdims=True)
        out.append(m)
        work = jnp.where(work == m, -jnp.inf, work)
    return out


SELECT_TOP_ROWS = 24
GATE_ROWS = 16


def _peer_select_kernel(a_ref, b_ref, thr_ref, ea_ref, eb_ref, a_top, b_top):
    k = PEER_TOPK
    sub = 8
    a_top[...] = jnp.full(a_top.shape, -jnp.inf, F32)
    b_top[...] = jnp.full(b_top.shape, -jnp.inf, F32)
    for h in range(PEER_HEADS):
        rows = slice(h * N_KEYS, (h + 1) * N_KEYS)
        a = a_ref[rows, :]
        b = b_ref[rows, :]
        a_sorted = _top_sorted(a, k + 1)
        b_sorted = _top_sorted(b, k + 1)
        for r in range(k + 1):
            a_top[r:r + 1, :] = a_sorted[r]
            b_top[r:r + 1, :] = b_sorted[r]
        slabs = [a_sorted[0] + b_top[...]]
        for r in range(1, sub):
            slabs.append(a_sorted[r] + b_top[0:sub, :])
        slabs.append(a_top[sub:, :] + b_sorted[0])
        top = _top_sorted(jnp.concatenate(slabs, axis=0), k + 1)
        z = jnp.ones_like(top[0])
        for r in range(1, k):
            z = z + jnp.exp(top[r] - top[0])
        tau = 0.5 * (top[k - 1] + top[k])
        thr_ref[:, h, :] = tau - a
        ea_ref[:, h, :] = jnp.exp(a - a_sorted[0])
        eb_ref[rows, :] = jnp.exp(b - b_sorted[0]) / z


def _peer_select(a_t, b_t, *, tn):
    rows, t = a_t.shape
    spec = pl.BlockSpec((rows, tn), lambda i: (0, i))
    out = jax.ShapeDtypeStruct((rows, t), F32)
    key_spec = pl.BlockSpec((N_KEYS, PEER_HEADS, tn), lambda i: (0, 0, i))
    key_out = jax.ShapeDtypeStruct((N_KEYS, PEER_HEADS, t), F32)
    return pl.pallas_call(
        _peer_select_kernel,
        grid=(t // tn,),
        in_specs=[spec, spec],
        out_specs=[key_spec, key_spec, spec],
        out_shape=[key_out, key_out, out],
        scratch_shapes=[pltpu.VMEM((SELECT_TOP_ROWS, tn), F32), pltpu.VMEM((SELECT_TOP_ROWS, tn), F32)],
        compiler_params=_params("parallel"),
        name="peer_select",
    )(a_t, b_t)


def _peer_kernel(h_ref, wd_ref, wu_ref, thr_ref, ea_ref, b_ref, eb_ref, o_ref, s_buf0, s_buf1, a_buf0, a_buf1,
                 *, n_blocks, n_exp_blocks):
    s = pl.program_id(0)
    n_slabs = wd_ref.shape[0] // N_KEYS

    @pl.when(s == 0)
    def _():
        for buf in (s_buf0, s_buf1, a_buf0, a_buf1, o_ref):
            buf[...] = jnp.zeros_like(buf)

    def step(s_cur, s_prv, a_cur, a_prv):
        first_key = (jnp.clip(s - 1, 0, n_blocks - 1) % n_exp_blocks) * n_slabs
        first = (jnp.maximum(s - 2, 0) % n_exp_blocks) == 0
        out_rows = o_ref.shape[0] // n_slabs
        for ii in range(n_slabs):
            rows = slice(ii * N_KEYS, (ii + 1) * N_KEYS)
            if ii % 2 == 0:
                pair = slice(ii * N_KEYS, (ii + 2) * N_KEYS)
                s_cur[pair, :] = jnp.dot(wd_ref[pair, :], h_ref[...], preferred_element_type=F32)

            key = first_key + ii
            for r in range(N_KEYS // GATE_ROWS):
                g = jnp.zeros((GATE_ROWS, s_prv.shape[1]), F32)
                for h in range(PEER_HEADS):
                    sub = slice(h * N_KEYS + r * GATE_ROWS, h * N_KEYS + (r + 1) * GATE_ROWS)
                    thr_row = thr_ref[key, pl.ds(h, 1), :]
                    ea_row = ea_ref[key, pl.ds(h, 1), :]
                    below = lax.bitcast_convert_type(thr_row - b_ref[sub, :], jnp.int32)
                    keep = lax.bitcast_convert_type(eb_ref[sub, :], jnp.int32) & (below >> 31)
                    g = g + lax.bitcast_convert_type(keep, F32) * ea_row
                srows = slice(ii * N_KEYS + r * GATE_ROWS, ii * N_KEYS + (r + 1) * GATE_ROWS)
                sv = s_prv[srows, :]
                act = 0.5 * sv * (1.0 + lax.erf(sv * INV_SQRT2))
                a_prv[srows, :] = (act * g).astype(a_prv.dtype)

            orow = slice(ii * out_rows, (ii + 1) * out_rows)
            upd = jnp.dot(wu_ref[orow, :], a_cur[...], preferred_element_type=F32)
            o_ref[orow, :] = jnp.where(first, upd, o_ref[orow, :] + upd)

    pl.when(s % 2 == 0)(functools.partial(step, s_buf0, s_buf1, a_buf0, a_buf1))
    pl.when(s % 2 == 1)(functools.partial(step, s_buf1, s_buf0, a_buf1, a_buf0))


def _peer_mixer(h_t, wd, wu_t, thr_t, ea_t, b_t, eb_t, *, tn, te):
    d, t = h_t.shape
    n_exp_blocks = wd.shape[0] // te
    n_blocks = (t // tn) * n_exp_blocks
    rows = b_t.shape[0]

    def blk(s, lag):
        return jnp.clip(s - lag, 0, n_blocks - 1)

    once = pl.Buffered(1)
    tok_spec = pl.BlockSpec((rows, tn), lambda s: (0, blk(s, 1) // n_exp_blocks), pipeline_mode=once)
    key_spec = pl.BlockSpec((N_KEYS, PEER_HEADS, tn), lambda s: (0, 0, blk(s, 1) // n_exp_blocks), pipeline_mode=once)
    return pl.pallas_call(
        functools.partial(_peer_kernel, n_blocks=n_blocks, n_exp_blocks=n_exp_blocks),
        grid=(n_blocks + 2,),
        in_specs=[
            pl.BlockSpec((d, tn), lambda s: (0, blk(s, 0) // n_exp_blocks), pipeline_mode=once),
            pl.BlockSpec((te, d), lambda s: (blk(s, 0) % n_exp_blocks, 0)),
            pl.BlockSpec((d, te), lambda s: (0, blk(s, 2) % n_exp_blocks)),
            key_spec, key_spec, tok_spec, tok_spec,
        ],
        out_specs=pl.BlockSpec((d, tn), lambda s: (0, blk(s, 2) // n_exp_blocks)),
        out_shape=jax.ShapeDtypeStruct((d, t), F32),
        scratch_shapes=[pltpu.VMEM((te, tn), F32), pltpu.VMEM((te, tn), F32),
                        pltpu.VMEM((te, tn), BF16), pltpu.VMEM((te, tn), BF16)],
        compiler_params=_params("arbitrary"),
        name="peer_mixer",
    )(h_t, wd, wu_t, thr_t, ea_t, b_t, eb_t)


def _final_kernel(x_ref, p_ref, g_ref, o_ref):
    x = x_ref[...] + p_ref[...].T
    ms = jnp.mean(x * x, axis=-1, keepdims=True)
    o_ref[...] = x * lax.rsqrt(ms + EPS) * g_ref[...]


def _final(x1, peer_t, g, *, tm):
    t, d = x1.shape
    return pl.pallas_call(
        _final_kernel,
        grid=(t // tm,),
        in_specs=[pl.BlockSpec((tm, d), lambda i: (i, 0)), pl.BlockSpec((d, tm), lambda i: (0, i)),
                  pl.BlockSpec((1, d), lambda i: (0, 0))],
        out_specs=pl.BlockSpec((tm, d), lambda i: (i, 0)),
        out_shape=jax.ShapeDtypeStruct((t, d), F32),
        compiler_params=_params("parallel"),
        name="final_norm",
    )(x1, peer_t, g.reshape(1, d))


def _tiles(t, seq):
    return dict(
        norm_tm=min(256, t),
        mm_tm=min(1024, t),
        mm_tn=1024,
        qk_tm=min(256, seq),
        attn_tq=min(512, seq),
        attn_tk=min(512, seq),
        mix_tm=min(256, seq),
        score_tn=min(1024, t),
        select_tn=min(256, t),
        peer_tn=min(512, t),
        peer_te=512,
    )


def kernel(x, norm_mix_g, w_in, q_norm_g, k_norm_g, conv_w, attn_out_g, conv_out_g, w_out, norm_ffn_g,
           peer_w_query, peer_sub_keys, peer_w_down, peer_w_up, norm_final_g):
    batch, seq, d = x.shape
    depth = w_in.shape[0]
    t = batch * seq
    tl = _tiles(t, seq)
    xf = x.reshape(t, d)
    for layer in range(depth):
        wi = w_in[layer]
        q_end, k_end = ATTN_WIDTH, ATTN_WIDTH + KV_WIDTH
        v_end = k_end + KV_WIDTH
        w_in_b = jnp.concatenate([wi[:, :q_end], wi[:, v_end:], wi[:, q_end:k_end], wi[:, k_end:v_end]],
                                 axis=1).astype(BF16)
        w_out_b = w_out[layer].astype(BF16)
        wq_t = peer_w_query[layer].T.astype(BF16)
        wd_b = peer_w_down[layer].astype(BF16)
        wu_t = peer_w_up[layer].T.astype(BF16)

        h = _rmsnorm(xf, norm_mix_g[layer], tm=tl["norm_tm"])
        proj = _matmul(h, w_in_b, tm=tl["mm_tm"], tn=tl["mm_tn"], out_dtype=BF16, name="in_proj")
        q, k = _qk_prep(proj, q_norm_g[layer], k_norm_g[layer], seq, tm=tl["qk_tm"])
        attn = _attention(q, k, proj, batch, seq, tq=tl["attn_tq"], tk=tl["attn_tk"])
        y = _mix_norm(attn, proj, conv_w[layer], attn_out_g[layer], conv_out_g[layer], seq, tm=tl["mix_tm"])
        x1 = _matmul(y, w_out_b, tm=tl["mm_tm"] // 2, tn=tl["mm_tn"], out_dtype=F32, residual=xf, name="out_proj")

        h2_t = _rmsnorm(x1, norm_ffn_g[layer], tm=tl["norm_tm"], transpose=True)
        a_t, b_t = _peer_scores(wq_t, h2_t, peer_sub_keys[layer], tn=tl["score_tn"])
        thr_t, ea_t, eb_t = _peer_select(a_t, b_t, tn=tl["select_tn"])
        peer_t = _peer_mixer(h2_t, wd_b, wu_t, thr_t, ea_t, b_t, eb_t, tn=tl["peer_tn"], te=tl["peer_te"])
        if layer == depth - 1:
            xf = _final(x1, peer_t, norm_final_g, tm=tl["norm_tm"])
        else:
            xf = x1 + peer_t.T
    return xf.reshape(batch, seq, d)
```

```python
import functools

import numpy as np
import jax
import jax.numpy as jnp
from jax import lax
from jax.experimental import pallas as pl
from jax.experimental.pallas import tpu as pltpu

F32 = jnp.float32
BF16 = jnp.bfloat16

HEAD_DIM = 128
N_Q_HEADS = 16
N_KV_HEADS = 4
GQA_GROUP = N_Q_HEADS // N_KV_HEADS
ATTN_WIDTH = N_Q_HEADS * HEAD_DIM
KV_WIDTH = N_KV_HEADS * HEAD_DIM
CONV_WIDTH = 2048
GRID_W = 64
ROPE_THETA = 10000.0
N_KEYS = 128
PEER_HEADS = 8
PEER_TOPK = 16
EPS = 1e-6
INV_SQRT2 = 0.7071067811865476
LOG2_E = 1.4426950408889634
Q_SCALE = HEAD_DIM ** -0.5 * LOG2_E

LANES = 128
BF16_SUBLANES = 16
VMEM_LIMIT_BYTES = 56 * 1024 * 1024

COL_Q = 0
COL_CX = ATTN_WIDTH
COL_GB = COL_CX + CONV_WIDTH
COL_GC = COL_GB + CONV_WIDTH
COL_K = COL_GC + CONV_WIDTH
COL_V = COL_K + KV_WIDTH
IN_COLS = COL_V + KV_WIDTH


def _params(*sem):
    return pltpu.CompilerParams(dimension_semantics=sem, vmem_limit_bytes=VMEM_LIMIT_BYTES)


def _rmsnorm_kernel(x_ref, g_ref, o_ref, *, transpose):
    x = x_ref[...]
    ms = jnp.mean(x * x, axis=-1, keepdims=True)
    y = x * lax.rsqrt(ms + EPS) * g_ref[...]
    if transpose:
        y = y.T
    o_ref[...] = y.astype(o_ref.dtype)


def _rmsnorm(x, g, *, tm, transpose=False):
    t, d = x.shape
    if transpose:
        out_shape = jax.ShapeDtypeStruct((d, t), BF16)
        out_spec = pl.BlockSpec((d, tm), lambda i: (0, i))
    else:
        out_shape = jax.ShapeDtypeStruct((t, d), BF16)
        out_spec = pl.BlockSpec((tm, d), lambda i: (i, 0))
    return pl.pallas_call(
        functools.partial(_rmsnorm_kernel, transpose=transpose),
        grid=(t // tm,),
        in_specs=[pl.BlockSpec((tm, d), lambda i: (i, 0)), pl.BlockSpec((1, d), lambda i: (0, 0))],
        out_specs=out_spec,
        out_shape=out_shape,
        compiler_params=_params("parallel"),
        name="rmsnorm_t" if transpose else "rmsnorm",
    )(x, g.reshape(1, d))


def _mm_kernel(a_ref, b_ref, o_ref):
    o_ref[...] = jnp.dot(a_ref[...], b_ref[...], preferred_element_type=F32).astype(o_ref.dtype)


def _mm_res_kernel(a_ref, b_ref, r_ref, o_ref):
    acc = jnp.dot(a_ref[...], b_ref[...], preferred_element_type=F32)
    o_ref[...] = (r_ref[...] + acc).astype(o_ref.dtype)


def _matmul(a, b, *, tm, tn, out_dtype, residual=None, name):
    m, k = a.shape
    _, n = b.shape
    in_specs = [pl.BlockSpec((tm, k), lambda i, j: (i, 0)), pl.BlockSpec((k, tn), lambda i, j: (0, j))]
    args = [a, b]
    body = _mm_kernel
    if residual is not None:
        in_specs.append(pl.BlockSpec((tm, tn), lambda i, j: (i, j)))
        args.append(residual)
        body = _mm_res_kernel
    return pl.pallas_call(
        body,
        grid=(m // tm, n // tn),
        in_specs=in_specs,
        out_specs=pl.BlockSpec((tm, tn), lambda i, j: (i, j)),
        out_shape=jax.ShapeDtypeStruct((m, n), out_dtype),
        compiler_params=_params("parallel", "parallel"),
        name=name,
    )(*args)


def _rope_tables(seq):
    t = np.arange(seq)
    half = HEAD_DIM // 2
    inv = ROPE_THETA ** (-np.arange(0, half, 2, dtype=np.float64) / half)
    ang_row = (t // GRID_W)[:, None] * inv
    ang_col = (t % GRID_W)[:, None] * inv
    zeros = np.zeros_like(ang_row)
    cos = np.concatenate([np.cos(ang_row)] * 2 + [np.cos(ang_col)] * 2, axis=1)
    sa = np.concatenate([-np.sin(ang_row), zeros, -np.sin(ang_col), zeros], axis=1)
    sb = np.concatenate([zeros, np.sin(ang_row), zeros, np.sin(ang_col)], axis=1)
    return (jnp.asarray(cos, F32), jnp.asarray(sa, F32), jnp.asarray(sb, F32))


def _qk_kernel(q_ref, k_ref, c_ref, sa_ref, sb_ref, gq_ref, gk_ref, qo_ref, ko_ref):
    c, sa, sb = c_ref[...], sa_ref[...], sb_ref[...]
    quarter = HEAD_DIM // 4

    def prep(x, g, scale):
        ms = jnp.mean(x * x, axis=-1, keepdims=True)
        y = x * lax.rsqrt(ms + EPS) * g
        r = y * c + pltpu.roll(y, HEAD_DIM - quarter, 1) * sa + pltpu.roll(y, quarter, 1) * sb
        return r * scale

    for h in range(N_Q_HEADS):
        sl = slice(h * HEAD_DIM, (h + 1) * HEAD_DIM)
        qo_ref[:, sl] = prep(q_ref[:, sl].astype(F32), gq_ref[...], Q_SCALE).astype(qo_ref.dtype)
    for h in range(N_KV_HEADS):
        sl = slice(h * HEAD_DIM, (h + 1) * HEAD_DIM)
        ko_ref[:, sl] = prep(k_ref[:, sl].astype(F32), gk_ref[...], 1.0).astype(ko_ref.dtype)


def _qk_prep(proj, gq, gk, seq, *, tm):
    t = proj.shape[0]
    tables = _rope_tables(seq)
    per_seq = seq // tm
    tab_spec = pl.BlockSpec((tm, HEAD_DIM), lambda i: (i % per_seq, 0))
    g_spec = pl.BlockSpec((1, HEAD_DIM), lambda i: (0, 0))
    return pl.pallas_call(
        _qk_kernel,
        grid=(t // tm,),
        in_specs=[
            pl.BlockSpec((tm, ATTN_WIDTH), lambda i: (i, COL_Q // ATTN_WIDTH)),
            pl.BlockSpec((tm, KV_WIDTH), lambda i: (i, COL_K // KV_WIDTH)),
            tab_spec, tab_spec, tab_spec, g_spec, g_spec,
        ],
        out_specs=[pl.BlockSpec((tm, ATTN_WIDTH), lambda i: (i, 0)), pl.BlockSpec((tm, KV_WIDTH), lambda i: (i, 0))],
        out_shape=[jax.ShapeDtypeStruct((t, ATTN_WIDTH), BF16), jax.ShapeDtypeStruct((t, KV_WIDTH), BF16)],
        compiler_params=_params("parallel"),
        name="qk_prep",
    )(proj, proj, *tables, gq.reshape(1, HEAD_DIM), gk.reshape(1, HEAD_DIM))


def _attn_kernel(q_ref, k_ref, v_ref, o_ref, *, tk):
    tq = q_ref.shape[0]
    n_chunks = k_ref.shape[0] // tk
    ones = jnp.ones((tk, HEAD_DIM), v_ref.dtype)
    for h in range(GQA_GROUP):
        sl = slice(h * HEAD_DIM, (h + 1) * HEAD_DIM)
        q = q_ref[:, sl]

        def body(i, carry, q=q):
            m, acc = carry
            off = pl.multiple_of(i * tk, tk)
            kc = k_ref[pl.ds(off, tk), :]
            vc = jnp.concatenate([v_ref[pl.ds(off, tk), :], ones], axis=1)
            s = lax.dot_general(q, kc, (((1,), (1,)), ((), ())), preferred_element_type=F32)
            m_new = jnp.maximum(m, jnp.max(s, axis=-1, keepdims=True))
            alpha = jnp.exp2(m - m_new)
            p = jnp.exp2(s - m_new)
            acc = alpha * acc + jnp.dot(p.astype(BF16), vc, preferred_element_type=F32)
            return m_new, acc

        init = (jnp.full((tq, 1), -jnp.inf, F32), jnp.zeros((tq, 2 * HEAD_DIM), F32))
        _, acc = lax.fori_loop(0, n_chunks, body, init, unroll=True)
        o_ref[:, sl] = (acc[:, :HEAD_DIM] / acc[:, HEAD_DIM:]).astype(o_ref.dtype)


def _attention(q, k, proj, batch, seq, *, tq, tk):
    t = q.shape[0]
    nq = seq // tq
    group_w = GQA_GROUP * HEAD_DIM
    return pl.pallas_call(
        functools.partial(_attn_kernel, tk=tk),
        grid=(batch, N_KV_HEADS, nq),
        in_specs=[
            pl.BlockSpec((tq, group_w), lambda b, g, i: (b * nq + i, g)),
            pl.BlockSpec((seq, HEAD_DIM), lambda b, g, i: (b, g)),
            pl.BlockSpec((seq, HEAD_DIM), lambda b, g, i: (b, COL_V // HEAD_DIM + g)),
        ],
        out_specs=pl.BlockSpec((tq, group_w), lambda b, g, i: (b * nq + i, g)),
        out_shape=jax.ShapeDtypeStruct((t, ATTN_WIDTH), BF16),
        compiler_params=_params("parallel", "parallel", "parallel"),
        name="attention",
    )(q, k, proj)


def _mix_kernel(attn_ref, cx_ref, gb_ref, gc_ref, cxp_ref, gcp_ref, cxn_ref, gcn_ref, w_ref, ga_ref, gv_ref,
                o_ref, *, per_seq):
    i = pl.program_id(0)
    tm = cx_ref.shape[0]
    u = gc_ref[...].astype(F32) * cx_ref[...].astype(F32)
    last_row = BF16_SUBLANES - 1
    prev = gcp_ref[last_row:, :].astype(F32) * cxp_ref[last_row:, :].astype(F32)
    nxt = gcn_ref[0:1, :].astype(F32) * cxn_ref[0:1, :].astype(F32)
    pos = i % per_seq
    prev = jnp.where(pos == 0, 0.0, prev)
    nxt = jnp.where(pos == per_seq - 1, 0.0, nxt)
    row = lax.broadcasted_iota(jnp.int32, (tm, 1), 0)
    u_dn = jnp.where(row == 0, prev, pltpu.roll(u, 1, 0))
    u_up = jnp.where(row == tm - 1, nxt, pltpu.roll(u, tm - 1, 0))
    w = w_ref[...]
    conv = gb_ref[...].astype(F32) * (u_dn * w[0:1] + u * w[1:2] + u_up * w[2:3])

    def norm(z, g):
        ms = jnp.mean(z * z, axis=-1, keepdims=True)
        return z * lax.rsqrt(ms + EPS) * g

    o_ref[:, :ATTN_WIDTH] = norm(attn_ref[...].astype(F32), ga_ref[...]).astype(o_ref.dtype)
    o_ref[:, ATTN_WIDTH:] = norm(conv, gv_ref[...]).astype(o_ref.dtype)


def _mix_norm(attn, proj, conv_w, ga, gv, seq, *, tm):
    t = attn.shape[0]
    per_seq = seq // tm
    halo = BF16_SUBLANES
    n_halo = t // halo
    cw = CONV_WIDTH

    def main(col):
        return pl.BlockSpec((tm, cw), lambda i: (i, col // cw))

    def prev(col):
        return pl.BlockSpec((halo, cw), lambda i: (jnp.maximum(i * (tm // halo) - 1, 0), col // cw))

    def nxt(col):
        return pl.BlockSpec((halo, cw), lambda i: (jnp.minimum((i + 1) * (tm // halo), n_halo - 1), col // cw))

    def vec(n):
        return pl.BlockSpec((n, cw), lambda i: (0, 0))

    return pl.pallas_call(
        functools.partial(_mix_kernel, per_seq=per_seq),
        grid=(t // tm,),
        in_specs=[pl.BlockSpec((tm, ATTN_WIDTH), lambda i: (i, 0)), main(COL_CX), main(COL_GB), main(COL_GC),
                  prev(COL_CX), prev(COL_GC), nxt(COL_CX), nxt(COL_GC), vec(3), vec(1), vec(1)],
        out_specs=pl.BlockSpec((tm, ATTN_WIDTH + cw), lambda i: (i, 0)),
        out_shape=jax.ShapeDtypeStruct((t, ATTN_WIDTH + cw), BF16),
        compiler_params=_params("parallel"),
        name="mix_norm",
    )(attn, proj, proj, proj, proj, proj, proj, proj, conv_w, ga.reshape(1, -1), gv.reshape(1, -1))


def _peer_scores_kernel(wq_ref, h_ref, sk_ref, a_ref, b_ref):
    qt = jnp.dot(wq_ref[...], h_ref[...], preferred_element_type=F32)
    for p, o_ref in enumerate((a_ref, b_ref)):
        rows = slice(p * N_KEYS, (p + 1) * N_KEYS)
        o_ref[...] = jnp.dot(sk_ref[0, p], qt[rows, :], preferred_element_type=F32,
                             precision=lax.Precision.HIGHEST)


def _peer_scores(wq_t, h_t, sub_keys, *, tn):
    d = wq_t.shape[1]
    t = h_t.shape[1]
    out = jax.ShapeDtypeStruct((PEER_HEADS * N_KEYS, t), F32)
    return pl.pallas_call(
        _peer_scores_kernel,
        grid=(t // tn, PEER_HEADS),
        in_specs=[
            pl.BlockSpec((2 * N_KEYS, d), lambda i, h: (h, 0)),
            pl.BlockSpec((d, tn), lambda i, h: (0, i)),
            pl.BlockSpec((1, 2, N_KEYS, N_KEYS), lambda i, h: (h, 0, 0, 0)),
        ],
        out_specs=[pl.BlockSpec((N_KEYS, tn), lambda i, h: (h, i))] * 2,
        out_shape=[out, out],
        compiler_params=_params("parallel", "parallel"),
        name="peer_scores",
    )(wq_t, h_t, sub_keys)


def _top_sorted(work, n):
    out = []
    for _ in range(n):
        m = jnp.max(work, axis=0, keepdims=True)
        out.append(m)
        work = jnp.where(work == m, -jnp.inf, work)
    return out


SELECT_TOP_ROWS = 24
UP_CHUNKS = 4
K_CHUNKS = 8
GATE_ROWS = 16


def _peer_select_kernel(a_ref, b_ref, thr_ref, ea_ref, eb_ref, a_top, b_top):
    k = PEER_TOPK
    sub = 8
    a_top[...] = jnp.full(a_top.shape, -jnp.inf, F32)
    b_top[...] = jnp.full(b_top.shape, -jnp.inf, F32)
    for h in range(PEER_HEADS):
        rows = slice(h * N_KEYS, (h + 1) * N_KEYS)
        a = a_ref[rows, :]
        b = b_ref[rows, :]
        a_sorted = _top_sorted(a, k + 1)
        b_sorted = _top_sorted(b, k + 1)
        for r in range(k + 1):
            a_top[r:r + 1, :] = a_sorted[r]
            b_top[r:r + 1, :] = b_sorted[r]
        slabs = [a_sorted[0] + b_top[...]]
        for r in range(1, sub):
            slabs.append(a_sorted[r] + b_top[0:sub, :])
        slabs.append(a_top[sub:, :] + b_sorted[0])
        top = _top_sorted(jnp.concatenate(slabs, axis=0), k + 1)
        z = jnp.ones_like(top[0])
        for r in range(1, k):
            z = z + jnp.exp(top[r] - top[0])
        tau = 0.5 * (top[k - 1] + top[k])
        thr_ref[:, h, :] = tau - a
        ea_ref[:, h, :] = jnp.exp(a - a_sorted[0])
        eb_ref[rows, :] = jnp.exp(b - b_sorted[0]) / z


def _peer_select(a_t, b_t, *, tn):
    rows, t = a_t.shape
    spec = pl.BlockSpec((rows, tn), lambda i: (0, i))
    out = jax.ShapeDtypeStruct((rows, t), F32)
    key_spec = pl.BlockSpec((N_KEYS, PEER_HEADS, tn), lambda i: (0, 0, i))
    key_out = jax.ShapeDtypeStruct((N_KEYS, PEER_HEADS, t), F32)
    return pl.pallas_call(
        _peer_select_kernel,
        grid=(t // tn,),
        in_specs=[spec, spec],
        out_specs=[key_spec, key_spec, spec],
        out_shape=[key_out, key_out, out],
        scratch_shapes=[pltpu.VMEM((SELECT_TOP_ROWS, tn), F32), pltpu.VMEM((SELECT_TOP_ROWS, tn), F32)],
        compiler_params=_params("parallel"),
        name="peer_select",
    )(a_t, b_t)


def _peer_kernel(h_ref, wd_ref, wu_ref, thr_ref, ea_ref, b_ref, eb_ref, o_ref, s_buf0, s_buf1, a_buf0, a_buf1,
                 *, n_blocks, n_exp_blocks):
    s = pl.program_id(0)
    n_slabs = wd_ref.shape[0] // N_KEYS

    @pl.when(s == 0)
    def _():
        for buf in (s_buf0, s_buf1, a_buf0, a_buf1, o_ref):
            buf[...] = jnp.zeros_like(buf)

    def step(s_cur, s_prv, a_cur, a_prv):
        first_key = (jnp.clip(s - 1, 0, n_blocks - 1) % n_exp_blocks) * n_slabs
        first = (jnp.maximum(s - 2, 0) % n_exp_blocks) == 0
        out_rows = o_ref.shape[0] // n_slabs
        tiles_per_slab = N_KEYS // GATE_ROWS
        k_chunk = h_ref.shape[0] // K_CHUNKS

        def gate_tile(ii, r, anchor):
            key = first_key + ii
            zero_bits = (lax.bitcast_convert_type(anchor, jnp.uint32) >> 16) >> 16
            g = lax.bitcast_convert_type(zero_bits, F32)
            for h in range(PEER_HEADS):
                sub = slice(h * N_KEYS + r * GATE_ROWS, h * N_KEYS + (r + 1) * GATE_ROWS)
                thr_row = thr_ref[key, pl.ds(h, 1), :]
                ea_row = ea_ref[key, pl.ds(h, 1), :]
                below = lax.bitcast_convert_type(thr_row - b_ref[sub, :], jnp.int32)
                keep = lax.bitcast_convert_type(eb_ref[sub, :], jnp.int32) & (below >> 31)
                g = g + lax.bitcast_convert_type(keep, F32) * ea_row
            srows = slice(ii * N_KEYS + r * GATE_ROWS, ii * N_KEYS + (r + 1) * GATE_ROWS)
            sv = s_prv[srows, :]
            act = 0.5 * sv * (1.0 + lax.erf(sv * INV_SQRT2))
            a_prv[srows, :] = (act * g).astype(a_prv.dtype)

        for ii in range(n_slabs):
            half = tiles_per_slab // 2
            per_up = half // UP_CHUNKS
            for u in range(UP_CHUNKS):
                up_rows = out_rows // UP_CHUNKS
                orow = slice(ii * out_rows + u * up_rows, ii * out_rows + (u + 1) * up_rows)
                upd = jnp.dot(wu_ref[orow, :], a_cur[...], preferred_element_type=F32)
                o_ref[orow, :] = jnp.where(first, upd, o_ref[orow, :] + upd)
                for r in range(per_up):
                    at = r * (up_rows // per_up)
                    gate_tile(ii, u * per_up + r, upd[at:at + GATE_ROWS, :])

            pair = slice((ii // 2) * 2 * N_KEYS, ((ii // 2) * 2 + 2) * N_KEYS)
            per_chunk = half // (K_CHUNKS // 2)
            for q in range(K_CHUNKS // 2):
                kc = (ii % 2) * (K_CHUNKS // 2) + q
                ks = slice(kc * k_chunk, (kc + 1) * k_chunk)
                part = jnp.dot(wd_ref[pair, ks], h_ref[ks, :], preferred_element_type=F32)
                if kc == 0:
                    s_cur[pair, :] = part
                else:
                    s_cur[pair, :] += part
                for r in range(per_chunk):
                    at = r * (2 * N_KEYS // per_chunk)
                    gate_tile(ii, half + q * per_chunk + r, part[at:at + GATE_ROWS, :])

    pl.when(s % 2 == 0)(functools.partial(step, s_buf0, s_buf1, a_buf0, a_buf1))
    pl.when(s % 2 == 1)(functools.partial(step, s_buf1, s_buf0, a_buf1, a_buf0))


def _peer_mixer(h_t, wd, wu_t, thr_t, ea_t, b_t, eb_t, *, tn, te):
    d, t = h_t.shape
    n_exp_blocks = wd.shape[0] // te
    n_blocks = (t // tn) * n_exp_blocks
    rows = b_t.shape[0]

    def blk(s, lag):
        return jnp.clip(s - lag, 0, n_blocks - 1)

    once = pl.Buffered(1)
    tok_spec = pl.BlockSpec((rows, tn), lambda s: (0, blk(s, 1) // n_exp_blocks), pipeline_mode=once)
    key_spec = pl.BlockSpec((N_KEYS, PEER_HEADS, tn), lambda s: (0, 0, blk(s, 1) // n_exp_blocks), pipeline_mode=once)
    return pl.pallas_call(
        functools.partial(_peer_kernel, n_blocks=n_blocks, n_exp_blocks=n_exp_blocks),
        grid=(n_blocks + 2,),
        in_specs=[
            pl.BlockSpec((d, tn), lambda s: (0, blk(s, 0) // n_exp_blocks), pipeline_mode=once),
            pl.BlockSpec((te, d), lambda s: (blk(s, 0) % n_exp_blocks, 0)),
            pl.BlockSpec((d, te), lambda s: (0, blk(s, 2) % n_exp_blocks)),
            key_spec, key_spec, tok_spec, tok_spec,
        ],
        out_specs=pl.BlockSpec((d, tn), lambda s: (0, blk(s, 2) // n_exp_blocks)),
        out_shape=jax.ShapeDtypeStruct((d, t), F32),
        scratch_shapes=[pltpu.VMEM((te, tn), F32), pltpu.VMEM((te, tn), F32),
                        pltpu.VMEM((te, tn), BF16), pltpu.VMEM((te, tn), BF16)],
        compiler_params=_params("arbitrary"),
        name="peer_mixer",
    )(h_t, wd, wu_t, thr_t, ea_t, b_t, eb_t)


def _final_kernel(x_ref, p_ref, g_ref, o_ref):
    x = x_ref[...] + p_ref[...].T
    ms = jnp.mean(x * x, axis=-1, keepdims=True)
    o_ref[...] = x * lax.rsqrt(ms + EPS) * g_ref[...]


def _final(x1, peer_t, g, *, tm):
    t, d = x1.shape
    return pl.pallas_call(
        _final_kernel,
        grid=(t // tm,),
        in_specs=[pl.BlockSpec((tm, d), lambda i: (i, 0)), pl.BlockSpec((d, tm), lambda i: (0, i)),
                  pl.BlockSpec((1, d), lambda i: (0, 0))],
        out_specs=pl.BlockSpec((tm, d), lambda i: (i, 0)),
        out_shape=jax.ShapeDtypeStruct((t, d), F32),
        compiler_params=_params("parallel"),
        name="final_norm",
    )(x1, peer_t, g.reshape(1, d))


def _tiles(t, seq):
    return dict(
        norm_tm=min(256, t),
        mm_tm=min(1024, t),
        mm_tn=1024,
        qk_tm=min(256, seq),
        attn_tq=min(512, seq),
        attn_tk=min(512, seq),
        mix_tm=min(256, seq),
        score_tn=min(1024, t),
        select_tn=min(256, t),
        peer_tn=min(512, t),
        peer_te=512,
    )


def kernel(x, norm_mix_g, w_in, q_norm_g, k_norm_g, conv_w, attn_out_g, conv_out_g, w_out, norm_ffn_g,
           peer_w_query, peer_sub_keys, peer_w_down, peer_w_up, norm_final_g):
    batch, seq, d = x.shape
    depth = w_in.shape[0]
    t = batch * seq
    tl = _tiles(t, seq)
    xf = x.reshape(t, d)
    for layer in range(depth):
        wi = w_in[layer]
        q_end, k_end = ATTN_WIDTH, ATTN_WIDTH + KV_WIDTH
        v_end = k_end + KV_WIDTH
        w_in_b = jnp.concatenate([wi[:, :q_end], wi[:, v_end:], wi[:, q_end:k_end], wi[:, k_end:v_end]],
                                 axis=1).astype(BF16)
        w_out_b = w_out[layer].astype(BF16)
        wq_t = peer_w_query[layer].T.astype(BF16)
        wd_b = peer_w_down[layer].astype(BF16)
        wu_t = peer_w_up[layer].T.astype(BF16)

        h = _rmsnorm(xf, norm_mix_g[layer], tm=tl["norm_tm"])
        proj = _matmul(h, w_in_b, tm=tl["mm_tm"], tn=tl["mm_tn"], out_dtype=BF16, name="in_proj")
        q, k = _qk_prep(proj, q_norm_g[layer], k_norm_g[layer], seq, tm=tl["qk_tm"])
        attn = _attention(q, k, proj, batch, seq, tq=tl["attn_tq"], tk=tl["attn_tk"])
        y = _mix_norm(attn, proj, conv_w[layer], attn_out_g[layer], conv_out_g[layer], seq, tm=tl["mix_tm"])
        x1 = _matmul(y, w_out_b, tm=tl["mm_tm"] // 2, tn=tl["mm_tn"], out_dtype=F32, residual=xf, name="out_proj")

        h2_t = _rmsnorm(x1, norm_ffn_g[layer], tm=tl["norm_tm"], transpose=True)
        a_t, b_t = _peer_scores(wq_t, h2_t, peer_sub_keys[layer], tn=tl["score_tn"])
        thr_t, ea_t, eb_t = _peer_select(a_t, b_t, tn=tl["select_tn"])
        peer_t = _peer_mixer(h2_t, wd_b, wu_t, thr_t, ea_t, b_t, eb_t, tn=tl["peer_tn"], te=tl["peer_te"])
        if layer == depth - 1:
            xf = _final(x1, peer_t, norm_final_g, tm=tl["norm_tm"])
        else:
            xf = x1 + peer_t.T
    return xf.reshape(batch, seq, d)
```

```python
import functools

import numpy as np
import jax
import jax.numpy as jnp
from jax import lax
from jax.experimental import pallas as pl
from jax.experimental.pallas import tpu as pltpu

F32 = jnp.float32
BF16 = jnp.bfloat16

HEAD_DIM = 128
N_Q_HEADS = 16
N_KV_HEADS = 4
GQA_GROUP = N_Q_HEADS // N_KV_HEADS
ATTN_WIDTH = N_Q_HEADS * HEAD_DIM
KV_WIDTH = N_KV_HEADS * HEAD_DIM
CONV_WIDTH = 2048
GRID_W = 64
ROPE_THETA = 10000.0
N_KEYS = 128
PEER_HEADS = 8
PEER_TOPK = 16
EPS = 1e-6
INV_SQRT2 = 0.7071067811865476
LOG2_E = 1.4426950408889634
Q_SCALE = HEAD_DIM ** -0.5 * LOG2_E

LANES = 128
BF16_SUBLANES = 16
VMEM_LIMIT_BYTES = 56 * 1024 * 1024

COL_Q = 0
COL_CX = ATTN_WIDTH
COL_GB = COL_CX + CONV_WIDTH
COL_GC = COL_GB + CONV_WIDTH
COL_K = COL_GC + CONV_WIDTH
COL_V = COL_K + KV_WIDTH
IN_COLS = COL_V + KV_WIDTH


def _params(*sem):
    return pltpu.CompilerParams(dimension_semantics=sem, vmem_limit_bytes=VMEM_LIMIT_BYTES)


def _rmsnorm_kernel(x_ref, g_ref, o_ref, *, transpose):
    x = x_ref[...]
    ms = jnp.mean(x * x, axis=-1, keepdims=True)
    y = x * lax.rsqrt(ms + EPS) * g_ref[...]
    if transpose:
        y = y.T
    o_ref[...] = y.astype(o_ref.dtype)


def _rmsnorm(x, g, *, tm, transpose=False):
    t, d = x.shape
    if transpose:
        out_shape = jax.ShapeDtypeStruct((d, t), BF16)
        out_spec = pl.BlockSpec((d, tm), lambda i: (0, i))
    else:
        out_shape = jax.ShapeDtypeStruct((t, d), BF16)
        out_spec = pl.BlockSpec((tm, d), lambda i: (i, 0))
    return pl.pallas_call(
        functools.partial(_rmsnorm_kernel, transpose=transpose),
        grid=(t // tm,),
        in_specs=[pl.BlockSpec((tm, d), lambda i: (i, 0)), pl.BlockSpec((1, d), lambda i: (0, 0))],
        out_specs=out_spec,
        out_shape=out_shape,
        compiler_params=_params("parallel"),
        name="rmsnorm_t" if transpose else "rmsnorm",
    )(x, g.reshape(1, d))


def _mm_kernel(a_ref, b_ref, o_ref):
    o_ref[...] = jnp.dot(a_ref[...], b_ref[...], preferred_element_type=F32).astype(o_ref.dtype)


def _mm_res_kernel(a_ref, b_ref, r_ref, o_ref):
    acc = jnp.dot(a_ref[...], b_ref[...], preferred_element_type=F32)
    o_ref[...] = (r_ref[...] + acc).astype(o_ref.dtype)


def _matmul(a, b, *, tm, tn, out_dtype, residual=None, name):
    m, k = a.shape
    _, n = b.shape
    in_specs = [pl.BlockSpec((tm, k), lambda i, j: (i, 0)), pl.BlockSpec((k, tn), lambda i, j: (0, j))]
    args = [a, b]
    body = _mm_kernel
    if residual is not None:
        in_specs.append(pl.BlockSpec((tm, tn), lambda i, j: (i, j)))
        args.append(residual)
        body = _mm_res_kernel
    return pl.pallas_call(
        body,
        grid=(m // tm, n // tn),
        in_specs=in_specs,
        out_specs=pl.BlockSpec((tm, tn), lambda i, j: (i, j)),
        out_shape=jax.ShapeDtypeStruct((m, n), out_dtype),
        compiler_params=_params("parallel", "parallel"),
        name=name,
    )(*args)


def _rope_tables(seq):
    t = np.arange(seq)
    half = HEAD_DIM // 2
    inv = ROPE_THETA ** (-np.arange(0, half, 2, dtype=np.float64) / half)
    ang_row = (t // GRID_W)[:, None] * inv
    ang_col = (t % GRID_W)[:, None] * inv
    zeros = np.zeros_like(ang_row)
    cos = np.concatenate([np.cos(ang_row)] * 2 + [np.cos(ang_col)] * 2, axis=1)
    sa = np.concatenate([-np.sin(ang_row), zeros, -np.sin(ang_col), zeros], axis=1)
    sb = np.concatenate([zeros, np.sin(ang_row), zeros, np.sin(ang_col)], axis=1)
    return (jnp.asarray(cos, F32), jnp.asarray(sa, F32), jnp.asarray(sb, F32))


def _qk_kernel(q_ref, k_ref, c_ref, sa_ref, sb_ref, gq_ref, gk_ref, qo_ref, ko_ref):
    c, sa, sb = c_ref[...], sa_ref[...], sb_ref[...]
    quarter = HEAD_DIM // 4

    def prep(x, g, scale):
        ms = jnp.mean(x * x, axis=-1, keepdims=True)
        y = x * lax.rsqrt(ms + EPS) * g
        r = y * c + pltpu.roll(y, HEAD_DIM - quarter, 1) * sa + pltpu.roll(y, quarter, 1) * sb
        return r * scale

    for h in range(N_Q_HEADS):
        sl = slice(h * HEAD_DIM, (h + 1) * HEAD_DIM)
        qo_ref[:, sl] = prep(q_ref[:, sl].astype(F32), gq_ref[...], Q_SCALE).astype(qo_ref.dtype)
    for h in range(N_KV_HEADS):
        sl = slice(h * HEAD_DIM, (h + 1) * HEAD_DIM)
        ko_ref[:, sl] = prep(k_ref[:, sl].astype(F32), gk_ref[...], 1.0).astype(ko_ref.dtype)


def _qk_prep(proj, gq, gk, seq, *, tm):
    t = proj.shape[0]
    tables = _rope_tables(seq)
    per_seq = seq // tm
    tab_spec = pl.BlockSpec((tm, HEAD_DIM), lambda i: (i % per_seq, 0))
    g_spec = pl.BlockSpec((1, HEAD_DIM), lambda i: (0, 0))
    return pl.pallas_call(
        _qk_kernel,
        grid=(t // tm,),
        in_specs=[
            pl.BlockSpec((tm, ATTN_WIDTH), lambda i: (i, COL_Q // ATTN_WIDTH)),
            pl.BlockSpec((tm, KV_WIDTH), lambda i: (i, COL_K // KV_WIDTH)),
            tab_spec, tab_spec, tab_spec, g_spec, g_spec,
        ],
        out_specs=[pl.BlockSpec((tm, ATTN_WIDTH), lambda i: (i, 0)), pl.BlockSpec((tm, KV_WIDTH), lambda i: (i, 0))],
        out_shape=[jax.ShapeDtypeStruct((t, ATTN_WIDTH), BF16), jax.ShapeDtypeStruct((t, KV_WIDTH), BF16)],
        compiler_params=_params("parallel"),
        name="qk_prep",
    )(proj, proj, *tables, gq.reshape(1, HEAD_DIM), gk.reshape(1, HEAD_DIM))


def _attn_kernel(q_ref, k_ref, v_ref, o_ref, *, tk):
    tq = q_ref.shape[0]
    n_chunks = k_ref.shape[0] // tk
    ones = jnp.ones((tk, HEAD_DIM), v_ref.dtype)
    for h in range(GQA_GROUP):
        sl = slice(h * HEAD_DIM, (h + 1) * HEAD_DIM)
        q = q_ref[:, sl]

        def body(i, carry, q=q):
            m, acc = carry
            off = pl.multiple_of(i * tk, tk)
            kc = k_ref[pl.ds(off, tk), :]
            vc = jnp.concatenate([v_ref[pl.ds(off, tk), :], ones], axis=1)
            s = lax.dot_general(q, kc, (((1,), (1,)), ((), ())), preferred_element_type=F32)
            m_new = jnp.maximum(m, jnp.max(s, axis=-1, keepdims=True))
            alpha = jnp.exp2(m - m_new)
            p = jnp.exp2(s - m_new)
            acc = alpha * acc + jnp.dot(p.astype(BF16), vc, preferred_element_type=F32)
            return m_new, acc

        init = (jnp.full((tq, 1), -jnp.inf, F32), jnp.zeros((tq, 2 * HEAD_DIM), F32))
        _, acc = lax.fori_loop(0, n_chunks, body, init, unroll=True)
        o_ref[:, sl] = (acc[:, :HEAD_DIM] / acc[:, HEAD_DIM:]).astype(o_ref.dtype)


def _attention(q, k, proj, batch, seq, *, tq, tk):
    t = q.shape[0]
    nq = seq // tq
    group_w = GQA_GROUP * HEAD_DIM
    return pl.pallas_call(
        functools.partial(_attn_kernel, tk=tk),
        grid=(batch, N_KV_HEADS, nq),
        in_specs=[
            pl.BlockSpec((tq, group_w), lambda b, g, i: (b * nq + i, g)),
            pl.BlockSpec((seq, HEAD_DIM), lambda b, g, i: (b, g)),
            pl.BlockSpec((seq, HEAD_DIM), lambda b, g, i: (b, COL_V // HEAD_DIM + g)),
        ],
        out_specs=pl.BlockSpec((tq, group_w), lambda b, g, i: (b * nq + i, g)),
        out_shape=jax.ShapeDtypeStruct((t, ATTN_WIDTH), BF16),
        compiler_params=_params("parallel", "parallel", "parallel"),
        name="attention",
    )(q, k, proj)


def _mix_kernel(attn_ref, cx_ref, gb_ref, gc_ref, cxp_ref, gcp_ref, cxn_ref, gcn_ref, w_ref, ga_ref, gv_ref,
                o_ref, *, per_seq):
    i = pl.program_id(0)
    tm = cx_ref.shape[0]
    u = gc_ref[...].astype(F32) * cx_ref[...].astype(F32)
    last_row = BF16_SUBLANES - 1
    prev = gcp_ref[last_row:, :].astype(F32) * cxp_ref[last_row:, :].astype(F32)
    nxt = gcn_ref[0:1, :].astype(F32) * cxn_ref[0:1, :].astype(F32)
    pos = i % per_seq
    prev = jnp.where(pos == 0, 0.0, prev)
    nxt = jnp.where(pos == per_seq - 1, 0.0, nxt)
    row = lax.broadcasted_iota(jnp.int32, (tm, 1), 0)
    u_dn = jnp.where(row == 0, prev, pltpu.roll(u, 1, 0))
    u_up = jnp.where(row == tm - 1, nxt, pltpu.roll(u, tm - 1, 0))
    w = w_ref[...]
    conv = gb_ref[...].astype(F32) * (u_dn * w[0:1] + u * w[1:2] + u_up * w[2:3])

    def norm(z, g):
        ms = jnp.mean(z * z, axis=-1, keepdims=True)
        return z * lax.rsqrt(ms + EPS) * g

    o_ref[:, :ATTN_WIDTH] = norm(attn_ref[...].astype(F32), ga_ref[...]).astype(o_ref.dtype)
    o_ref[:, ATTN_WIDTH:] = norm(conv, gv_ref[...]).astype(o_ref.dtype)


def _mix_norm(attn, proj, conv_w, ga, gv, seq, *, tm):
    t = attn.shape[0]
    per_seq = seq // tm
    halo = BF16_SUBLANES
    n_halo = t // halo
    cw = CONV_WIDTH

    def main(col):
        return pl.BlockSpec((tm, cw), lambda i: (i, col // cw))

    def prev(col):
        return pl.BlockSpec((halo, cw), lambda i: (jnp.maximum(i * (tm // halo) - 1, 0), col // cw))

    def nxt(col):
        return pl.BlockSpec((halo, cw), lambda i: (jnp.minimum((i + 1) * (tm // halo), n_halo - 1), col // cw))

    def vec(n):
        return pl.BlockSpec((n, cw), lambda i: (0, 0))

    return pl.pallas_call(
        functools.partial(_mix_kernel, per_seq=per_seq),
        grid=(t // tm,),
        in_specs=[pl.BlockSpec((tm, ATTN_WIDTH), lambda i: (i, 0)), main(COL_CX), main(COL_GB), main(COL_GC),
                  prev(COL_CX), prev(COL_GC), nxt(COL_CX), nxt(COL_GC), vec(3), vec(1), vec(1)],
        out_specs=pl.BlockSpec((tm, ATTN_WIDTH + cw), lambda i: (i, 0)),
        out_shape=jax.ShapeDtypeStruct((t, ATTN_WIDTH + cw), BF16),
        compiler_params=_params("parallel"),
        name="mix_norm",
    )(attn, proj, proj, proj, proj, proj, proj, proj, conv_w, ga.reshape(1, -1), gv.reshape(1, -1))


def _peer_scores_kernel(wq_ref, h_ref, sk_ref, a_ref, b_ref):
    qt = jnp.dot(wq_ref[...], h_ref[...], preferred_element_type=F32)

    def split(v):
        hi = v.astype(BF16)
        return hi, (v - hi.astype(F32)).astype(BF16)

    q_hi, q_lo = split(qt)
    for h in range(PEER_HEADS):
        for p, o_ref in enumerate((a_ref, b_ref)):
            rows = slice((2 * h + p) * N_KEYS, (2 * h + p + 1) * N_KEYS)
            k_hi, k_lo = split(sk_ref[h, p])
            lhs = jnp.concatenate([k_hi, k_hi, k_lo], axis=1)
            rhs = jnp.concatenate([q_hi[rows, :], q_lo[rows, :], q_hi[rows, :]], axis=0)
            o_ref[h * N_KEYS:(h + 1) * N_KEYS, :] = jnp.dot(lhs, rhs, preferred_element_type=F32)


def _peer_scores(wq_t, h_t, sub_keys, *, tn):
    rows, d = wq_t.shape
    t = h_t.shape[1]
    out = jax.ShapeDtypeStruct((PEER_HEADS * N_KEYS, t), F32)
    once = pl.Buffered(1)
    return pl.pallas_call(
        _peer_scores_kernel,
        grid=(t // tn,),
        in_specs=[
            pl.BlockSpec((rows, d), lambda i: (0, 0), pipeline_mode=once),
            pl.BlockSpec((d, tn), lambda i: (0, i)),
            pl.BlockSpec(sub_keys.shape, lambda i: (0, 0, 0, 0), pipeline_mode=once),
        ],
        out_specs=[pl.BlockSpec((PEER_HEADS * N_KEYS, tn), lambda i: (0, i))] * 2,
        out_shape=[out, out],
        compiler_params=_params("parallel"),
        name="peer_scores",
    )(wq_t, h_t, sub_keys)


def _top_sorted(work, n):
    out = []
    for _ in range(n):
        m = jnp.max(work, axis=0, keepdims=True)
        out.append(m)
        work = jnp.where(work == m, -jnp.inf, work)
    return out


SELECT_TOP_ROWS = 24
UP_CHUNKS = 4
K_CHUNKS = 8
GATE_ROWS = 16


def _peer_select_kernel(a_ref, b_ref, thr_ref, ea_ref, eb_ref, a_top, b_top):
    k = PEER_TOPK
    sub = 8
    a_top[...] = jnp.full(a_top.shape, -jnp.inf, F32)
    b_top[...] = jnp.full(b_top.shape, -jnp.inf, F32)
    for h in range(PEER_HEADS):
        rows = slice(h * N_KEYS, (h + 1) * N_KEYS)
        a = a_ref[rows, :]
        b = b_ref[rows, :]
        a_sorted = _top_sorted(a, k + 1)
        b_sorted = _top_sorted(b, k + 1)
        for r in range(k + 1):
            a_top[r:r + 1, :] = a_sorted[r]
            b_top[r:r + 1, :] = b_sorted[r]
        slabs = [a_sorted[0] + b_top[...]]
        for r in range(1, sub):
            slabs.append(a_sorted[r] + b_top[0:sub, :])
        slabs.append(a_top[sub:, :] + b_sorted[0])
        top = _top_sorted(jnp.concatenate(slabs, axis=0), k + 1)
        z = jnp.ones_like(top[0])
        for r in range(1, k):
            z = z + jnp.exp(top[r] - top[0])
        tau = 0.5 * (top[k - 1] + top[k])
        thr_ref[:, h, :] = tau - a
        ea_ref[:, h, :] = jnp.exp(a - a_sorted[0])
        eb_ref[rows, :] = jnp.exp(b - b_sorted[0]) / z


def _peer_select(a_t, b_t, *, tn):
    rows, t = a_t.shape
    spec = pl.BlockSpec((rows, tn), lambda i: (0, i))
    out = jax.ShapeDtypeStruct((rows, t), F32)
    key_spec = pl.BlockSpec((N_KEYS, PEER_HEADS, tn), lambda i: (0, 0, i))
    key_out = jax.ShapeDtypeStruct((N_KEYS, PEER_HEADS, t), F32)
    return pl.pallas_call(
        _peer_select_kernel,
        grid=(t // tn,),
        in_specs=[spec, spec],
        out_specs=[key_spec, key_spec, spec],
        out_shape=[key_out, key_out, out],
        scratch_shapes=[pltpu.VMEM((SELECT_TOP_ROWS, tn), F32), pltpu.VMEM((SELECT_TOP_ROWS, tn), F32)],
        compiler_params=_params("parallel"),
        name="peer_select",
    )(a_t, b_t)


def _peer_kernel(h_ref, wd_ref, wu_ref, thr_ref, ea_ref, b_ref, eb_ref, o_ref, s_buf0, s_buf1, a_buf0, a_buf1,
                 *, n_blocks, n_exp_blocks):
    s = pl.program_id(0)
    n_slabs = wd_ref.shape[0] // N_KEYS

    @pl.when(s == 0)
    def _():
        for buf in (s_buf0, s_buf1, a_buf0, a_buf1):
            buf[...] = jnp.zeros_like(buf)

    @pl.when((jnp.maximum(s - 2, 0) % n_exp_blocks) == 0)
    def _():
        o_ref[...] = jnp.zeros_like(o_ref)

    def step(s_cur, s_prv, a_cur, a_prv):
        first_key = (jnp.clip(s - 1, 0, n_blocks - 1) % n_exp_blocks) * n_slabs
        out_rows = o_ref.shape[0] // n_slabs
        tiles_per_slab = N_KEYS // GATE_ROWS
        k_chunk = h_ref.shape[0] // K_CHUNKS

        def gate_tile(ii, r, anchor):
            key = first_key + ii
            zero_bits = (lax.bitcast_convert_type(anchor, jnp.uint32) >> 16) >> 16
            g = lax.bitcast_convert_type(zero_bits, F32)
            for h in range(PEER_HEADS):
                sub = slice(h * N_KEYS + r * GATE_ROWS, h * N_KEYS + (r + 1) * GATE_ROWS)
                thr_row = thr_ref[key, pl.ds(h, 1), :]
                ea_row = ea_ref[key, pl.ds(h, 1), :]
                below = lax.bitcast_convert_type(thr_row - b_ref[sub, :], jnp.int32)
                keep = lax.bitcast_convert_type(eb_ref[sub, :], jnp.int32) & (below >> 31)
                g = g + lax.bitcast_convert_type(keep, F32) * ea_row
            srows = slice(ii * N_KEYS + r * GATE_ROWS, ii * N_KEYS + (r + 1) * GATE_ROWS)
            sv = s_prv[srows, :]
            act = 0.5 * sv * (1.0 + lax.erf(sv * INV_SQRT2))
            a_prv[srows, :] = (act * g).astype(a_prv.dtype)

        for ii in range(n_slabs):
            half = tiles_per_slab // 2
            per_up = half // UP_CHUNKS
            for u in range(UP_CHUNKS):
                up_rows = out_rows // UP_CHUNKS
                orow = slice(ii * out_rows + u * up_rows, ii * out_rows + (u + 1) * up_rows)
                upd = jnp.dot(wu_ref[orow, :], a_cur[...], preferred_element_type=F32)
                o_ref[orow, :] += upd
                for r in range(per_up):
                    at = r * (up_rows // per_up)
                    gate_tile(ii, u * per_up + r, upd[at:at + GATE_ROWS, :])

            pair = slice((ii // 2) * 2 * N_KEYS, ((ii // 2) * 2 + 2) * N_KEYS)
            per_chunk = half // (K_CHUNKS // 2)
            for q in range(K_CHUNKS // 2):
                kc = (ii % 2) * (K_CHUNKS // 2) + q
                ks = slice(kc * k_chunk, (kc + 1) * k_chunk)
                part = jnp.dot(wd_ref[pair, ks], h_ref[ks, :], preferred_element_type=F32)
                if kc == 0:
                    s_cur[pair, :] = part
                else:
                    s_cur[pair, :] += part
                for r in range(per_chunk):
                    at = r * (2 * N_KEYS // per_chunk)
                    gate_tile(ii, half + q * per_chunk + r, part[at:at + GATE_ROWS, :])

    pl.when(s % 2 == 0)(functools.partial(step, s_buf0, s_buf1, a_buf0, a_buf1))
    pl.when(s % 2 == 1)(functools.partial(step, s_buf1, s_buf0, a_buf1, a_buf0))


def _peer_mixer(h_t, wd, wu_t, thr_t, ea_t, b_t, eb_t, *, tn, te):
    d, t = h_t.shape
    n_exp_blocks = wd.shape[0] // te
    n_blocks = (t // tn) * n_exp_blocks
    rows = b_t.shape[0]

    def blk(s, lag):
        return jnp.clip(s - lag, 0, n_blocks - 1)

    once = pl.Buffered(1)
    tok_spec = pl.BlockSpec((rows, tn), lambda s: (0, blk(s, 1) // n_exp_blocks), pipeline_mode=once)
    key_spec = pl.BlockSpec((N_KEYS, PEER_HEADS, tn), lambda s: (0, 0, blk(s, 1) // n_exp_blocks), pipeline_mode=once)
    return pl.pallas_call(
        functools.partial(_peer_kernel, n_blocks=n_blocks, n_exp_blocks=n_exp_blocks),
        grid=(n_blocks + 2,),
        in_specs=[
            pl.BlockSpec((d, tn), lambda s: (0, blk(s, 0) // n_exp_blocks), pipeline_mode=once),
            pl.BlockSpec((te, d), lambda s: (blk(s, 0) % n_exp_blocks, 0)),
            pl.BlockSpec((d, te), lambda s: (0, blk(s, 2) % n_exp_blocks)),
            key_spec, key_spec, tok_spec, tok_spec,
        ],
        out_specs=pl.BlockSpec((d, tn), lambda s: (0, blk(s, 2) // n_exp_blocks)),
        out_shape=jax.ShapeDtypeStruct((d, t), F32),
        scratch_shapes=[pltpu.VMEM((te, tn), F32), pltpu.VMEM((te, tn), F32),
                        pltpu.VMEM((te, tn), BF16), pltpu.VMEM((te, tn), BF16)],
        compiler_params=_params("arbitrary"),
        name="peer_mixer",
    )(h_t, wd, wu_t, thr_t, ea_t, b_t, eb_t)


def _final_kernel(x_ref, p_ref, g_ref, o_ref):
    x = x_ref[...] + p_ref[...].T
    ms = jnp.mean(x * x, axis=-1, keepdims=True)
    o_ref[...] = x * lax.rsqrt(ms + EPS) * g_ref[...]


def _final(x1, peer_t, g, *, tm):
    t, d = x1.shape
    return pl.pallas_call(
        _final_kernel,
        grid=(t // tm,),
        in_specs=[pl.BlockSpec((tm, d), lambda i: (i, 0)), pl.BlockSpec((d, tm), lambda i: (0, i)),
                  pl.BlockSpec((1, d), lambda i: (0, 0))],
        out_specs=pl.BlockSpec((tm, d), lambda i: (i, 0)),
        out_shape=jax.ShapeDtypeStruct((t, d), F32),
        compiler_params=_params("parallel"),
        name="final_norm",
    )(x1, peer_t, g.reshape(1, d))


def _tiles(t, seq):
    return dict(
        norm_tm=min(256, t),
        mm_tm=min(1024, t),
        mm_tn=1024,
        qk_tm=min(256, seq),
        attn_tq=min(512, seq),
        attn_tk=min(512, seq),
        mix_tm=min(256, seq),
        score_tn=min(512, t),
        select_tn=min(256, t),
        peer_tn=min(512, t),
        peer_te=512,
    )


def kernel(x, norm_mix_g, w_in, q_norm_g, k_norm_g, conv_w, attn_out_g, conv_out_g, w_out, norm_ffn_g,
           peer_w_query, peer_sub_keys, peer_w_down, peer_w_up, norm_final_g):
    batch, seq, d = x.shape
    assert w_in.shape[0] == 1, "one layer: the final norm is fused with the layer's PEER residual"
    t = batch * seq
    tl = _tiles(t, seq)
    xf = x.reshape(t, d)

    wi = w_in[0]
    q_end, k_end = ATTN_WIDTH, ATTN_WIDTH + KV_WIDTH
    v_end = k_end + KV_WIDTH
    w_in_b = jnp.concatenate([wi[:, :q_end], wi[:, v_end:], wi[:, q_end:k_end], wi[:, k_end:v_end]],
                             axis=1).astype(BF16)
    w_out_b = w_out[0].astype(BF16)
    wq_t = peer_w_query[0].astype(BF16).T
    wd_b = peer_w_down[0].astype(BF16)
    wu_t = peer_w_up[0].astype(BF16).T

    h = _rmsnorm(xf, norm_mix_g[0], tm=tl["norm_tm"])
    proj = _matmul(h, w_in_b, tm=tl["mm_tm"], tn=tl["mm_tn"], out_dtype=BF16, name="in_proj")
    q, k = _qk_prep(proj, q_norm_g[0], k_norm_g[0], seq, tm=tl["qk_tm"])
    attn = _attention(q, k, proj, batch, seq, tq=tl["attn_tq"], tk=tl["attn_tk"])
    y = _mix_norm(attn, proj, conv_w[0], attn_out_g[0], conv_out_g[0], seq, tm=tl["mix_tm"])
    x1 = _matmul(y, w_out_b, tm=tl["mm_tm"], tn=tl["mm_tn"], out_dtype=F32, residual=xf, name="out_proj")

    h2_t = _rmsnorm(x1, norm_ffn_g[0], tm=tl["norm_tm"], transpose=True)
    a_t, b_t = _peer_scores(wq_t, h2_t, peer_sub_keys[0], tn=tl["score_tn"])
    thr_t, ea_t, eb_t = _peer_select(a_t, b_t, tn=tl["select_tn"])
    peer_t = _peer_mixer(h2_t, wd_b, wu_t, thr_t, ea_t, b_t, eb_t, tn=tl["peer_tn"], te=tl["peer_te"])
    out = _final(x1, peer_t, norm_final_g, tm=tl["norm_tm"])
    return out.reshape(batch, seq, d)
```

```python
import functools

import numpy as np
import jax
import jax.numpy as jnp
from jax import lax
from jax.experimental import pallas as pl
from jax.experimental.pallas import tpu as pltpu

F32 = jnp.float32
BF16 = jnp.bfloat16

HEAD_DIM = 128
N_Q_HEADS = 16
N_KV_HEADS = 4
GQA_GROUP = N_Q_HEADS // N_KV_HEADS
ATTN_WIDTH = N_Q_HEADS * HEAD_DIM
KV_WIDTH = N_KV_HEADS * HEAD_DIM
CONV_WIDTH = 2048
GRID_W = 64
ROPE_THETA = 10000.0
N_KEYS = 128
PEER_HEADS = 8
PEER_TOPK = 16
EPS = 1e-6
INV_SQRT2 = 0.7071067811865476
LOG2_E = 1.4426950408889634
Q_SCALE = HEAD_DIM ** -0.5 * LOG2_E

LANES = 128
BF16_SUBLANES = 16
VMEM_LIMIT_BYTES = 56 * 1024 * 1024

COL_Q = 0
COL_CX = ATTN_WIDTH
COL_GB = COL_CX + CONV_WIDTH
COL_GC = COL_GB + CONV_WIDTH
COL_K = COL_GC + CONV_WIDTH
COL_V = COL_K + KV_WIDTH
IN_COLS = COL_V + KV_WIDTH


def _params(*sem):
    return pltpu.CompilerParams(dimension_semantics=sem, vmem_limit_bytes=VMEM_LIMIT_BYTES)


def _rmsnorm_kernel(x_ref, g_ref, o_ref, *, transpose):
    x = x_ref[...]
    ms = jnp.mean(x * x, axis=-1, keepdims=True)
    y = x * lax.rsqrt(ms + EPS) * g_ref[...]
    if transpose:
        y = y.T
    o_ref[...] = y.astype(o_ref.dtype)


def _rmsnorm(x, g, *, tm, transpose=False):
    t, d = x.shape
    if transpose:
        out_shape = jax.ShapeDtypeStruct((d, t), BF16)
        out_spec = pl.BlockSpec((d, tm), lambda i: (0, i))
    else:
        out_shape = jax.ShapeDtypeStruct((t, d), BF16)
        out_spec = pl.BlockSpec((tm, d), lambda i: (i, 0))
    return pl.pallas_call(
        functools.partial(_rmsnorm_kernel, transpose=transpose),
        grid=(t // tm,),
        in_specs=[pl.BlockSpec((tm, d), lambda i: (i, 0)), pl.BlockSpec((1, d), lambda i: (0, 0))],
        out_specs=out_spec,
        out_shape=out_shape,
        compiler_params=_params("parallel"),
        name="rmsnorm_t" if transpose else "rmsnorm",
    )(x, g.reshape(1, d))


def _cast_t_kernel(w_ref, o_ref):
    o_ref[...] = w_ref[...].T.astype(o_ref.dtype).reshape(o_ref.shape)


def _cast_transpose(w, *, tm, blocked=False):
    rows, cols = w.shape
    if blocked:
        out_spec = pl.BlockSpec((1, cols, tm), lambda i: (i, 0, 0))
        out_shape = jax.ShapeDtypeStruct((rows // tm, cols, tm), BF16)
    else:
        out_spec = pl.BlockSpec((cols, tm), lambda i: (0, i))
        out_shape = jax.ShapeDtypeStruct((cols, rows), BF16)
    return pl.pallas_call(
        _cast_t_kernel,
        grid=(rows // tm,),
        in_specs=[pl.BlockSpec((tm, cols), lambda i: (i, 0))],
        out_specs=out_spec,
        out_shape=out_shape,
        compiler_params=_params("parallel"),
        name="cast_transpose",
    )(w)


def _mm_kernel(a_ref, b_ref, o_ref):
    o_ref[...] = jnp.dot(a_ref[...], b_ref[...], preferred_element_type=F32).astype(o_ref.dtype)


def _mm_res_kernel(a_ref, b_ref, r_ref, o_ref):
    acc = jnp.dot(a_ref[...], b_ref[...], preferred_element_type=F32)
    o_ref[...] = (r_ref[...] + acc).astype(o_ref.dtype)


def _matmul(a, b, *, tm, tn, out_dtype, residual=None, name):
    m, k = a.shape
    _, n = b.shape
    in_specs = [pl.BlockSpec((tm, k), lambda i, j: (i, 0)), pl.BlockSpec((k, tn), lambda i, j: (0, j))]
    args = [a, b]
    body = _mm_kernel
    if residual is not None:
        in_specs.append(pl.BlockSpec((tm, tn), lambda i, j: (i, j)))
        args.append(residual)
        body = _mm_res_kernel
    return pl.pallas_call(
        body,
        grid=(m // tm, n // tn),
        in_specs=in_specs,
        out_specs=pl.BlockSpec((tm, tn), lambda i, j: (i, j)),
        out_shape=jax.ShapeDtypeStruct((m, n), out_dtype),
        compiler_params=_params("parallel", "parallel"),
        name=name,
    )(*args)


def _rope_tables(seq):
    t = np.arange(seq)
    half = HEAD_DIM // 2
    inv = ROPE_THETA ** (-np.arange(0, half, 2, dtype=np.float64) / half)
    ang_row = (t // GRID_W)[:, None] * inv
    ang_col = (t % GRID_W)[:, None] * inv
    zeros = np.zeros_like(ang_row)
    cos = np.concatenate([np.cos(ang_row)] * 2 + [np.cos(ang_col)] * 2, axis=1)
    sa = np.concatenate([-np.sin(ang_row), zeros, -np.sin(ang_col), zeros], axis=1)
    sb = np.concatenate([zeros, np.sin(ang_row), zeros, np.sin(ang_col)], axis=1)
    return (jnp.asarray(cos, F32), jnp.asarray(sa, F32), jnp.asarray(sb, F32))


def _qk_kernel(q_ref, k_ref, c_ref, sa_ref, sb_ref, gq_ref, gk_ref, qo_ref, ko_ref):
    c, sa, sb = c_ref[...], sa_ref[...], sb_ref[...]
    quarter = HEAD_DIM // 4

    def prep(x, g, scale):
        ms = jnp.mean(x * x, axis=-1, keepdims=True)
        y = x * lax.rsqrt(ms + EPS) * g
        r = y * c + pltpu.roll(y, HEAD_DIM - quarter, 1) * sa + pltpu.roll(y, quarter, 1) * sb
        return r * scale

    for h in range(N_Q_HEADS):
        sl = slice(h * HEAD_DIM, (h + 1) * HEAD_DIM)
        qo_ref[:, sl] = prep(q_ref[:, sl].astype(F32), gq_ref[...], Q_SCALE).astype(qo_ref.dtype)
    for h in range(N_KV_HEADS):
        sl = slice(h * HEAD_DIM, (h + 1) * HEAD_DIM)
        ko_ref[:, sl] = prep(k_ref[:, sl].astype(F32), gk_ref[...], 1.0).astype(ko_ref.dtype)


def _qk_prep(proj, gq, gk, seq, *, tm):
    t = proj.shape[0]
    tables = _rope_tables(seq)
    per_seq = seq // tm
    tab_spec = pl.BlockSpec((tm, HEAD_DIM), lambda i: (i % per_seq, 0))
    g_spec = pl.BlockSpec((1, HEAD_DIM), lambda i: (0, 0))
    return pl.pallas_call(
        _qk_kernel,
        grid=(t // tm,),
        in_specs=[
            pl.BlockSpec((tm, ATTN_WIDTH), lambda i: (i, COL_Q // ATTN_WIDTH)),
            pl.BlockSpec((tm, KV_WIDTH), lambda i: (i, COL_K // KV_WIDTH)),
            tab_spec, tab_spec, tab_spec, g_spec, g_spec,
        ],
        out_specs=[pl.BlockSpec((tm, ATTN_WIDTH), lambda i: (i, 0)), pl.BlockSpec((tm, KV_WIDTH), lambda i: (i, 0))],
        out_shape=[jax.ShapeDtypeStruct((t, ATTN_WIDTH), BF16), jax.ShapeDtypeStruct((t, KV_WIDTH), BF16)],
        compiler_params=_params("parallel"),
        name="qk_prep",
    )(proj, proj, *tables, gq.reshape(1, HEAD_DIM), gk.reshape(1, HEAD_DIM))


def _attn_kernel(q_ref, k_ref, v_ref, o_ref, *, tk):
    tq = q_ref.shape[0]
    n_chunks = k_ref.shape[0] // tk
    ones = jnp.ones((tk, HEAD_DIM), v_ref.dtype)
    for h in range(GQA_GROUP):
        sl = slice(h * HEAD_DIM, (h + 1) * HEAD_DIM)
        q = q_ref[:, sl]

        def body(i, carry, q=q):
            m, acc = carry
            off = pl.multiple_of(i * tk, tk)
            kc = k_ref[pl.ds(off, tk), :]
            vc = jnp.concatenate([v_ref[pl.ds(off, tk), :], ones], axis=1)
            s = lax.dot_general(q, kc, (((1,), (1,)), ((), ())), preferred_element_type=F32)
            m_new = jnp.maximum(m, jnp.max(s, axis=-1, keepdims=True))
            alpha = jnp.exp2(m - m_new)
            p = jnp.exp2(s - m_new)
            acc = alpha * acc + jnp.dot(p.astype(BF16), vc, preferred_element_type=F32)
            return m_new, acc

        init = (jnp.full((tq, 1), -jnp.inf, F32), jnp.zeros((tq, 2 * HEAD_DIM), F32))
        _, acc = lax.fori_loop(0, n_chunks, body, init, unroll=True)
        o_ref[:, sl] = (acc[:, :HEAD_DIM] / acc[:, HEAD_DIM:]).astype(o_ref.dtype)


def _attention(q, k, proj, batch, seq, *, tq, tk):
    t = q.shape[0]
    nq = seq // tq
    group_w = GQA_GROUP * HEAD_DIM
    return pl.pallas_call(
        functools.partial(_attn_kernel, tk=tk),
        grid=(batch, N_KV_HEADS, nq),
        in_specs=[
            pl.BlockSpec((tq, group_w), lambda b, g, i: (b * nq + i, g)),
            pl.BlockSpec((seq, HEAD_DIM), lambda b, g, i: (b, g)),
            pl.BlockSpec((seq, HEAD_DIM), lambda b, g, i: (b, COL_V // HEAD_DIM + g)),
        ],
        out_specs=pl.BlockSpec((tq, group_w), lambda b, g, i: (b * nq + i, g)),
        out_shape=jax.ShapeDtypeStruct((t, ATTN_WIDTH), BF16),
        compiler_params=_params("parallel", "parallel", "parallel"),
        name="attention",
    )(q, k, proj)


def _mix_kernel(attn_ref, cx_ref, gb_ref, gc_ref, cxp_ref, gcp_ref, cxn_ref, gcn_ref, w_ref, ga_ref, gv_ref,
                o_ref, *, per_seq):
    i = pl.program_id(0)
    tm = cx_ref.shape[0]
    u = gc_ref[...].astype(F32) * cx_ref[...].astype(F32)
    last_row = BF16_SUBLANES - 1
    prev = gcp_ref[last_row:, :].astype(F32) * cxp_ref[last_row:, :].astype(F32)
    nxt = gcn_ref[0:1, :].astype(F32) * cxn_ref[0:1, :].astype(F32)
    pos = i % per_seq
    prev = jnp.where(pos == 0, 0.0, prev)
    nxt = jnp.where(pos == per_seq - 1, 0.0, nxt)
    row = lax.broadcasted_iota(jnp.int32, (tm, 1), 0)
    u_dn = jnp.where(row == 0, prev, pltpu.roll(u, 1, 0))
    u_up = jnp.where(row == tm - 1, nxt, pltpu.roll(u, tm - 1, 0))
    w = w_ref[...]
    conv = gb_ref[...].astype(F32) * (u_dn * w[0:1] + u * w[1:2] + u_up * w[2:3])

    def norm(z, g):
        ms = jnp.mean(z * z, axis=-1, keepdims=True)
        return z * lax.rsqrt(ms + EPS) * g

    o_ref[:, :ATTN_WIDTH] = norm(attn_ref[...].astype(F32), ga_ref[...]).astype(o_ref.dtype)
    o_ref[:, ATTN_WIDTH:] = norm(conv, gv_ref[...]).astype(o_ref.dtype)


def _mix_norm(attn, proj, conv_w, ga, gv, seq, *, tm):
    t = attn.shape[0]
    per_seq = seq // tm
    halo = BF16_SUBLANES
    n_halo = t // halo
    cw = CONV_WIDTH

    def main(col):
        return pl.BlockSpec((tm, cw), lambda i: (i, col // cw))

    def prev(col):
        return pl.BlockSpec((halo, cw), lambda i: (jnp.maximum(i * (tm // halo) - 1, 0), col // cw))

    def nxt(col):
        return pl.BlockSpec((halo, cw), lambda i: (jnp.minimum((i + 1) * (tm // halo), n_halo - 1), col // cw))

    def vec(n):
        return pl.BlockSpec((n, cw), lambda i: (0, 0))

    return pl.pallas_call(
        functools.partial(_mix_kernel, per_seq=per_seq),
        grid=(t // tm,),
        in_specs=[pl.BlockSpec((tm, ATTN_WIDTH), lambda i: (i, 0)), main(COL_CX), main(COL_GB), main(COL_GC),
                  prev(COL_CX), prev(COL_GC), nxt(COL_CX), nxt(COL_GC), vec(3), vec(1), vec(1)],
        out_specs=pl.BlockSpec((tm, ATTN_WIDTH + cw), lambda i: (i, 0)),
        out_shape=jax.ShapeDtypeStruct((t, ATTN_WIDTH + cw), BF16),
        compiler_params=_params("parallel"),
        name="mix_norm",
    )(attn, proj, proj, proj, proj, proj, proj, proj, conv_w, ga.reshape(1, -1), gv.reshape(1, -1))


def _peer_scores_kernel(wq_ref, h_ref, sk_ref, a_ref, b_ref):
    qt = jnp.dot(wq_ref[...], h_ref[...], preferred_element_type=F32)

    def split(v):
        hi = v.astype(BF16)
        return hi, (v - hi.astype(F32)).astype(BF16)

    q_hi, q_lo = split(qt)
    for h in range(PEER_HEADS):
        for p, o_ref in enumerate((a_ref, b_ref)):
            rows = slice((2 * h + p) * N_KEYS, (2 * h + p + 1) * N_KEYS)
            k_hi, k_lo = split(sk_ref[h, p])
            lhs = jnp.concatenate([k_hi, k_hi, k_lo], axis=1)
            rhs = jnp.concatenate([q_hi[rows, :], q_lo[rows, :], q_hi[rows, :]], axis=0)
            o_ref[h * N_KEYS:(h + 1) * N_KEYS, :] = jnp.dot(lhs, rhs, preferred_element_type=F32)


def _peer_scores(wq_t, h_t, sub_keys, *, tn):
    rows, d = wq_t.shape
    t = h_t.shape[1]
    out = jax.ShapeDtypeStruct((PEER_HEADS * N_KEYS, t), F32)
    once = pl.Buffered(1)
    return pl.pallas_call(
        _peer_scores_kernel,
        grid=(t // tn,),
        in_specs=[
            pl.BlockSpec((rows, d), lambda i: (0, 0), pipeline_mode=once),
            pl.BlockSpec((d, tn), lambda i: (0, i)),
            pl.BlockSpec(sub_keys.shape, lambda i: (0, 0, 0, 0), pipeline_mode=once),
        ],
        out_specs=[pl.BlockSpec((PEER_HEADS * N_KEYS, tn), lambda i: (0, i))] * 2,
        out_shape=[out, out],
        compiler_params=_params("parallel"),
        name="peer_scores",
    )(wq_t, h_t, sub_keys)


def _top_sorted(work, n):
    out = []
    for _ in range(n):
        m = jnp.max(work, axis=0, keepdims=True)
        out.append(m)
        work = jnp.where(work == m, -jnp.inf, work)
    return out


SELECT_TOP_ROWS = 24
UP_CHUNKS = 4
K_CHUNKS = 8
GATE_ROWS = 16


def _peer_select_kernel(a_ref, b_ref, thr_ref, ea_ref, eb_ref, a_top, b_top):
    k = PEER_TOPK
    sub = 8
    a_top[...] = jnp.full(a_top.shape, -jnp.inf, F32)
    b_top[...] = jnp.full(b_top.shape, -jnp.inf, F32)
    for h in range(PEER_HEADS):
        rows = slice(h * N_KEYS, (h + 1) * N_KEYS)
        a = a_ref[rows, :]
        b = b_ref[rows, :]
        a_sorted = _top_sorted(a, k + 1)
        b_sorted = _top_sorted(b, k + 1)
        for r in range(k + 1):
            a_top[r:r + 1, :] = a_sorted[r]
            b_top[r:r + 1, :] = b_sorted[r]
        slabs = [a_sorted[0] + b_top[...]]
        for r in range(1, sub):
            slabs.append(a_sorted[r] + b_top[0:sub, :])
        slabs.append(a_top[sub:, :] + b_sorted[0])
        top = _top_sorted(jnp.concatenate(slabs, axis=0), k + 1)
        z = jnp.ones_like(top[0])
        for r in range(1, k):
            z = z + jnp.exp(top[r] - top[0])
        tau = 0.5 * (top[k - 1] + top[k])
        thr_ref[:, h, :] = tau - a
        ea_ref[:, h, :] = jnp.exp(a - a_sorted[0])
        eb_ref[rows, :] = jnp.exp(b - b_sorted[0]) / z


def _peer_select(a_t, b_t, *, tn):
    rows, t = a_t.shape
    spec = pl.BlockSpec((rows, tn), lambda i: (0, i))
    out = jax.ShapeDtypeStruct((rows, t), F32)
    key_spec = pl.BlockSpec((N_KEYS, PEER_HEADS, tn), lambda i: (0, 0, i))
    key_out = jax.ShapeDtypeStruct((N_KEYS, PEER_HEADS, t), F32)
    return pl.pallas_call(
        _peer_select_kernel,
        grid=(t // tn,),
        in_specs=[spec, spec],
        out_specs=[key_spec, key_spec, spec],
        out_shape=[key_out, key_out, out],
        scratch_shapes=[pltpu.VMEM((SELECT_TOP_ROWS, tn), F32), pltpu.VMEM((SELECT_TOP_ROWS, tn), F32)],
        compiler_params=_params("parallel"),
        name="peer_select",
    )(a_t, b_t)


def _peer_kernel(h_ref, wd_ref, wu_ref, thr_ref, ea_ref, b_ref, eb_ref, o_ref, s_buf0, s_buf1, a_buf0, a_buf1,
                 *, n_blocks, n_exp_blocks):
    s = pl.program_id(0)
    n_slabs = wd_ref.shape[0] // N_KEYS

    @pl.when(s == 0)
    def _():
        for buf in (s_buf0, s_buf1, a_buf0, a_buf1):
            buf[...] = jnp.zeros_like(buf)

    @pl.when((jnp.maximum(s - 2, 0) % n_exp_blocks) == 0)
    def _():
        o_ref[...] = jnp.zeros_like(o_ref)

    def step(s_cur, s_prv, a_cur, a_prv):
        first_key = (jnp.clip(s - 1, 0, n_blocks - 1) % n_exp_blocks) * n_slabs
        out_rows = o_ref.shape[0] // n_slabs
        tiles_per_slab = N_KEYS // GATE_ROWS
        k_chunk = h_ref.shape[0] // K_CHUNKS

        def gate_tile(ii, r, anchor):
            key = first_key + ii
            zero_bits = (lax.bitcast_convert_type(anchor, jnp.uint32) >> 16) >> 16
            g = lax.bitcast_convert_type(zero_bits, F32)
            for h in range(PEER_HEADS):
                sub = slice(h * N_KEYS + r * GATE_ROWS, h * N_KEYS + (r + 1) * GATE_ROWS)
                thr_row = thr_ref[key, pl.ds(h, 1), :]
                ea_row = ea_ref[key, pl.ds(h, 1), :]
                below = lax.bitcast_convert_type(thr_row - b_ref[sub, :], jnp.int32)
                keep = lax.bitcast_convert_type(eb_ref[sub, :], jnp.int32) & (below >> 31)
                g = g + lax.bitcast_convert_type(keep, F32) * ea_row
            srows = slice(ii * N_KEYS + r * GATE_ROWS, ii * N_KEYS + (r + 1) * GATE_ROWS)
            sv = s_prv[srows, :]
            act = 0.5 * sv * (1.0 + lax.erf(sv * INV_SQRT2))
            a_prv[srows, :] = (act * g).astype(a_prv.dtype)

        for ii in range(n_slabs):
            half = tiles_per_slab // 2
            per_up = half // UP_CHUNKS
            for u in range(UP_CHUNKS):
                up_rows = out_rows // UP_CHUNKS
                orow = slice(ii * out_rows + u * up_rows, ii * out_rows + (u + 1) * up_rows)
                upd = jnp.dot(wu_ref[orow, :], a_cur[...], preferred_element_type=F32)
                o_ref[orow, :] += upd
                for r in range(per_up):
                    at = r * (up_rows // per_up)
                    gate_tile(ii, u * per_up + r, upd[at:at + GATE_ROWS, :])

            pair = slice((ii // 2) * 2 * N_KEYS, ((ii // 2) * 2 + 2) * N_KEYS)
            per_chunk = half // (K_CHUNKS // 2)
            for q in range(K_CHUNKS // 2):
                kc = (ii % 2) * (K_CHUNKS // 2) + q
                ks = slice(kc * k_chunk, (kc + 1) * k_chunk)
                part = jnp.dot(wd_ref[pair, ks], h_ref[ks, :], preferred_element_type=F32)
                if kc == 0:
                    s_cur[pair, :] = part
                else:
                    s_cur[pair, :] += part
                for r in range(per_chunk):
                    at = r * (2 * N_KEYS // per_chunk)
                    gate_tile(ii, half + q * per_chunk + r, part[at:at + GATE_ROWS, :])

    pl.when(s % 2 == 0)(functools.partial(step, s_buf0, s_buf1, a_buf0, a_buf1))
    pl.when(s % 2 == 1)(functools.partial(step, s_buf1, s_buf0, a_buf1, a_buf0))


def _peer_mixer(h_t, wd, wu_blk, thr_t, ea_t, b_t, eb_t, *, tn):
    d, t = h_t.shape
    n_exp_blocks, _, te = wu_blk.shape
    n_blocks = (t // tn) * n_exp_blocks
    rows = b_t.shape[0]

    def blk(s, lag):
        return jnp.clip(s - lag, 0, n_blocks - 1)

    once = pl.Buffered(1)
    tok_spec = pl.BlockSpec((rows, tn), lambda s: (0, blk(s, 1) // n_exp_blocks), pipeline_mode=once)
    key_spec = pl.BlockSpec((N_KEYS, PEER_HEADS, tn), lambda s: (0, 0, blk(s, 1) // n_exp_blocks), pipeline_mode=once)
    return pl.pallas_call(
        functools.partial(_peer_kernel, n_blocks=n_blocks, n_exp_blocks=n_exp_blocks),
        grid=(n_blocks + 2,),
        in_specs=[
            pl.BlockSpec((d, tn), lambda s: (0, blk(s, 0) // n_exp_blocks), pipeline_mode=once),
            pl.BlockSpec((te, d), lambda s: (blk(s, 0) % n_exp_blocks, 0)),
            pl.BlockSpec((None, d, te), lambda s: (blk(s, 2) % n_exp_blocks, 0, 0)),
            key_spec, key_spec, tok_spec, tok_spec,
        ],
        out_specs=pl.BlockSpec((d, tn), lambda s: (0, blk(s, 2) // n_exp_blocks)),
        out_shape=jax.ShapeDtypeStruct((d, t), F32),
        scratch_shapes=[pltpu.VMEM((te, tn), F32), pltpu.VMEM((te, tn), F32),
                        pltpu.VMEM((te, tn), BF16), pltpu.VMEM((te, tn), BF16)],
        compiler_params=_params("arbitrary"),
        name="peer_mixer",
    )(h_t, wd, wu_blk, thr_t, ea_t, b_t, eb_t)


def _final_kernel(x_ref, p_ref, g_ref, o_ref):
    x = x_ref[...] + p_ref[...].T
    ms = jnp.mean(x * x, axis=-1, keepdims=True)
    o_ref[...] = x * lax.rsqrt(ms + EPS) * g_ref[...]


def _final(x1, peer_t, g, *, tm):
    t, d = x1.shape
    return pl.pallas_call(
        _final_kernel,
        grid=(t // tm,),
        in_specs=[pl.BlockSpec((tm, d), lambda i: (i, 0)), pl.BlockSpec((d, tm), lambda i: (0, i)),
                  pl.BlockSpec((1, d), lambda i: (0, 0))],
        out_specs=pl.BlockSpec((tm, d), lambda i: (i, 0)),
        out_shape=jax.ShapeDtypeStruct((t, d), F32),
        compiler_params=_params("parallel"),
        name="final_norm",
    )(x1, peer_t, g.reshape(1, d))


def _tiles(t, seq):
    return dict(
        norm_tm=min(256, t),
        mm_tm=min(1024, t),
        mm_tn=1024,
        qk_tm=min(256, seq),
        attn_tq=min(512, seq),
        attn_tk=min(512, seq),
        mix_tm=min(256, seq),
        score_tn=min(512, t),
        select_tn=min(256, t),
        peer_tn=min(512, t),
        peer_te=512,
    )


def kernel(x, norm_mix_g, w_in, q_norm_g, k_norm_g, conv_w, attn_out_g, conv_out_g, w_out, norm_ffn_g,
           peer_w_query, peer_sub_keys, peer_w_down, peer_w_up, norm_final_g):
    batch, seq, d = x.shape
    assert w_in.shape[0] == 1, "one layer: the final norm is fused with the layer's PEER residual"
    t = batch * seq
    tl = _tiles(t, seq)
    xf = x.reshape(t, d)

    wi = w_in[0]
    q_end, k_end = ATTN_WIDTH, ATTN_WIDTH + KV_WIDTH
    v_end = k_end + KV_WIDTH
    w_in_b = jnp.concatenate([wi[:, :q_end], wi[:, v_end:], wi[:, q_end:k_end], wi[:, k_end:v_end]],
                             axis=1).astype(BF16)
    w_out_b = w_out[0].astype(BF16)
    wq_t = _cast_transpose(peer_w_query[0], tm=tl["norm_tm"])
    wd_b = peer_w_down[0].astype(BF16)
    wu_blk = _cast_transpose(peer_w_up[0], tm=tl["peer_te"], blocked=True)

    h = _rmsnorm(xf, norm_mix_g[0], tm=tl["norm_tm"])
    proj = _matmul(h, w_in_b, tm=tl["mm_tm"], tn=tl["mm_tn"], out_dtype=BF16, name="in_proj")
    q, k = _qk_prep(proj, q_norm_g[0], k_norm_g[0], seq, tm=tl["qk_tm"])
    attn = _attention(q, k, proj, batch, seq, tq=tl["attn_tq"], tk=tl["attn_tk"])
    y = _mix_norm(attn, proj, conv_w[0], attn_out_g[0], conv_out_g[0], seq, tm=tl["mix_tm"])
    x1 = _matmul(y, w_out_b, tm=tl["mm_tm"], tn=tl["mm_tn"], out_dtype=F32, residual=xf, name="out_proj")

    h2_t = _rmsnorm(x1, norm_ffn_g[0], tm=tl["norm_tm"], transpose=True)
    a_t, b_t = _peer_scores(wq_t, h2_t, peer_sub_keys[0], tn=tl["score_tn"])
    thr_t, ea_t, eb_t = _peer_select(a_t, b_t, tn=tl["select_tn"])
    peer_t = _peer_mixer(h2_t, wd_b, wu_blk, thr_t, ea_t, b_t, eb_t, tn=tl["peer_tn"])
    out = _final(x1, peer_t, norm_final_g, tm=tl["norm_tm"])
    return out.reshape(batch, seq, d)
```

```python
import functools

import numpy as np
import jax
import jax.numpy as jnp
from jax import lax
from jax.experimental import pallas as pl
from jax.experimental.pallas import tpu as pltpu

F32 = jnp.float32
BF16 = jnp.bfloat16

HEAD_DIM = 128
N_Q_HEADS = 16
N_KV_HEADS = 4
GQA_GROUP = N_Q_HEADS // N_KV_HEADS
ATTN_WIDTH = N_Q_HEADS * HEAD_DIM
KV_WIDTH = N_KV_HEADS * HEAD_DIM
CONV_WIDTH = 2048
GRID_W = 64
ROPE_THETA = 10000.0
N_KEYS = 128
PEER_HEADS = 8
PEER_TOPK = 16
EPS = 1e-6
INV_SQRT2 = 0.7071067811865476
LOG2_E = 1.4426950408889634
Q_SCALE = HEAD_DIM ** -0.5 * LOG2_E

LANES = 128
BF16_SUBLANES = 16
VMEM_LIMIT_BYTES = 56 * 1024 * 1024

COL_Q = 0
COL_CX = ATTN_WIDTH
COL_GB = COL_CX + CONV_WIDTH
COL_GC = COL_GB + CONV_WIDTH
COL_K = COL_GC + CONV_WIDTH
COL_V = COL_K + KV_WIDTH
IN_COLS = COL_V + KV_WIDTH


def _params(*sem):
    return pltpu.CompilerParams(dimension_semantics=sem, vmem_limit_bytes=VMEM_LIMIT_BYTES)


def _rmsnorm_kernel(x_ref, g_ref, o_ref, *, transpose):
    x = x_ref[...]
    ms = jnp.mean(x * x, axis=-1, keepdims=True)
    y = x * lax.rsqrt(ms + EPS) * g_ref[...]
    if transpose:
        y = y.T
    o_ref[...] = y.astype(o_ref.dtype)


def _rmsnorm(x, g, *, tm, transpose=False):
    t, d = x.shape
    if transpose:
        out_shape = jax.ShapeDtypeStruct((d, t), BF16)
        out_spec = pl.BlockSpec((d, tm), lambda i: (0, i))
    else:
        out_shape = jax.ShapeDtypeStruct((t, d), BF16)
        out_spec = pl.BlockSpec((tm, d), lambda i: (i, 0))
    return pl.pallas_call(
        functools.partial(_rmsnorm_kernel, transpose=transpose),
        grid=(t // tm,),
        in_specs=[pl.BlockSpec((tm, d), lambda i: (i, 0)), pl.BlockSpec((1, d), lambda i: (0, 0))],
        out_specs=out_spec,
        out_shape=out_shape,
        compiler_params=_params("parallel"),
        name="rmsnorm_t" if transpose else "rmsnorm",
    )(x, g.reshape(1, d))


def _cast_t_kernel(w_ref, o_ref):
    o_ref[...] = w_ref[...].T.astype(o_ref.dtype).reshape(o_ref.shape)


def _cast_transpose(w, *, tm, blocked=False):
    rows, cols = w.shape
    if blocked:
        out_spec = pl.BlockSpec((1, cols, tm), lambda i: (i, 0, 0))
        out_shape = jax.ShapeDtypeStruct((rows // tm, cols, tm), BF16)
    else:
        out_spec = pl.BlockSpec((cols, tm), lambda i: (0, i))
        out_shape = jax.ShapeDtypeStruct((cols, rows), BF16)
    return pl.pallas_call(
        _cast_t_kernel,
        grid=(rows // tm,),
        in_specs=[pl.BlockSpec((tm, cols), lambda i: (i, 0))],
        out_specs=out_spec,
        out_shape=out_shape,
        compiler_params=_params("parallel"),
        name="cast_transpose",
    )(w)


def _mm_kernel(a_ref, b_ref, o_ref):
    o_ref[...] = jnp.dot(a_ref[...], b_ref[...], preferred_element_type=F32).astype(o_ref.dtype)


def _mm_res_kernel(a_ref, b_ref, r_ref, o_ref):
    acc = jnp.dot(a_ref[...], b_ref[...], preferred_element_type=F32)
    o_ref[...] = (r_ref[...] + acc).astype(o_ref.dtype)


def _matmul(a, b, *, tm, tn, out_dtype, residual=None, col_block=lambda j: j, name):
    m, k = a.shape
    _, n = b.shape
    in_specs = [pl.BlockSpec((tm, k), lambda i, j: (i, 0)), pl.BlockSpec((k, tn), lambda i, j: (0, col_block(j)))]
    args = [a, b]
    body = _mm_kernel
    if residual is not None:
        in_specs.append(pl.BlockSpec((tm, tn), lambda i, j: (i, j)))
        args.append(residual)
        body = _mm_res_kernel
    return pl.pallas_call(
        body,
        grid=(m // tm, n // tn),
        in_specs=in_specs,
        out_specs=pl.BlockSpec((tm, tn), lambda i, j: (i, j)),
        out_shape=jax.ShapeDtypeStruct((m, n), out_dtype),
        compiler_params=_params("parallel", "parallel"),
        name=name,
    )(*args)


def _rope_tables(seq):
    t = np.arange(seq)
    half = HEAD_DIM // 2
    inv = ROPE_THETA ** (-np.arange(0, half, 2, dtype=np.float64) / half)
    ang_row = (t // GRID_W)[:, None] * inv
    ang_col = (t % GRID_W)[:, None] * inv
    zeros = np.zeros_like(ang_row)
    cos = np.concatenate([np.cos(ang_row)] * 2 + [np.cos(ang_col)] * 2, axis=1)
    sa = np.concatenate([-np.sin(ang_row), zeros, -np.sin(ang_col), zeros], axis=1)
    sb = np.concatenate([zeros, np.sin(ang_row), zeros, np.sin(ang_col)], axis=1)
    return (jnp.asarray(cos, F32), jnp.asarray(sa, F32), jnp.asarray(sb, F32))


def _qk_kernel(q_ref, k_ref, c_ref, sa_ref, sb_ref, gq_ref, gk_ref, qo_ref, ko_ref):
    c, sa, sb = c_ref[...], sa_ref[...], sb_ref[...]
    quarter = HEAD_DIM // 4

    def prep(x, g, scale):
        ms = jnp.mean(x * x, axis=-1, keepdims=True)
        y = x * lax.rsqrt(ms + EPS) * g
        r = y * c + pltpu.roll(y, HEAD_DIM - quarter, 1) * sa + pltpu.roll(y, quarter, 1) * sb
        return r * scale

    for h in range(N_Q_HEADS):
        sl = slice(h * HEAD_DIM, (h + 1) * HEAD_DIM)
        qo_ref[:, sl] = prep(q_ref[:, sl].astype(F32), gq_ref[...], Q_SCALE).astype(qo_ref.dtype)
    for h in range(N_KV_HEADS):
        sl = slice(h * HEAD_DIM, (h + 1) * HEAD_DIM)
        ko_ref[:, sl] = prep(k_ref[:, sl].astype(F32), gk_ref[...], 1.0).astype(ko_ref.dtype)


def _qk_prep(proj, gq, gk, seq, *, tm):
    t = proj.shape[0]
    tables = _rope_tables(seq)
    per_seq = seq // tm
    tab_spec = pl.BlockSpec((tm, HEAD_DIM), lambda i: (i % per_seq, 0))
    g_spec = pl.BlockSpec((1, HEAD_DIM), lambda i: (0, 0))
    return pl.pallas_call(
        _qk_kernel,
        grid=(t // tm,),
        in_specs=[
            pl.BlockSpec((tm, ATTN_WIDTH), lambda i: (i, COL_Q // ATTN_WIDTH)),
            pl.BlockSpec((tm, KV_WIDTH), lambda i: (i, COL_K // KV_WIDTH)),
            tab_spec, tab_spec, tab_spec, g_spec, g_spec,
        ],
        out_specs=[pl.BlockSpec((tm, ATTN_WIDTH), lambda i: (i, 0)), pl.BlockSpec((tm, KV_WIDTH), lambda i: (i, 0))],
        out_shape=[jax.ShapeDtypeStruct((t, ATTN_WIDTH), BF16), jax.ShapeDtypeStruct((t, KV_WIDTH), BF16)],
        compiler_params=_params("parallel"),
        name="qk_prep",
    )(proj, proj, *tables, gq.reshape(1, HEAD_DIM), gk.reshape(1, HEAD_DIM))


def _attn_kernel(q_ref, k_ref, v_ref, o_ref, *, tk):
    tq = q_ref.shape[0]
    n_chunks = k_ref.shape[0] // tk
    ones = jnp.ones((tk, HEAD_DIM), v_ref.dtype)
    for h in range(GQA_GROUP):
        sl = slice(h * HEAD_DIM, (h + 1) * HEAD_DIM)
        q = q_ref[:, sl]

        def body(i, carry, q=q):
            m, acc = carry
            off = pl.multiple_of(i * tk, tk)
            kc = k_ref[pl.ds(off, tk), :]
            vc = jnp.concatenate([v_ref[pl.ds(off, tk), :], ones], axis=1)
            s = lax.dot_general(q, kc, (((1,), (1,)), ((), ())), preferred_element_type=F32)
            m_new = jnp.maximum(m, jnp.max(s, axis=-1, keepdims=True))
            alpha = jnp.exp2(m - m_new)
            p = jnp.exp2(s - m_new)
            acc = alpha * acc + jnp.dot(p.astype(BF16), vc, preferred_element_type=F32)
            return m_new, acc

        init = (jnp.full((tq, 1), -jnp.inf, F32), jnp.zeros((tq, 2 * HEAD_DIM), F32))
        _, acc = lax.fori_loop(0, n_chunks, body, init, unroll=True)
        o_ref[:, sl] = (acc[:, :HEAD_DIM] / acc[:, HEAD_DIM:]).astype(o_ref.dtype)


def _attention(q, k, proj, batch, seq, *, tq, tk):
    t = q.shape[0]
    nq = seq // tq
    group_w = GQA_GROUP * HEAD_DIM
    return pl.pallas_call(
        functools.partial(_attn_kernel, tk=tk),
        grid=(batch, N_KV_HEADS, nq),
        in_specs=[
            pl.BlockSpec((tq, group_w), lambda b, g, i: (b * nq + i, g)),
            pl.BlockSpec((seq, HEAD_DIM), lambda b, g, i: (b, g)),
            pl.BlockSpec((seq, HEAD_DIM), lambda b, g, i: (b, COL_V // HEAD_DIM + g)),
        ],
        out_specs=pl.BlockSpec((tq, group_w), lambda b, g, i: (b * nq + i, g)),
        out_shape=jax.ShapeDtypeStruct((t, ATTN_WIDTH), BF16),
        compiler_params=_params("parallel", "parallel", "parallel"),
        name="attention",
    )(q, k, proj)


def _mix_kernel(attn_ref, cx_ref, gb_ref, gc_ref, cxp_ref, gcp_ref, cxn_ref, gcn_ref, w_ref, ga_ref, gv_ref,
                o_ref, *, per_seq):
    i = pl.program_id(0)
    tm = cx_ref.shape[0]
    u = gc_ref[...].astype(F32) * cx_ref[...].astype(F32)
    last_row = BF16_SUBLANES - 1
    prev = gcp_ref[last_row:, :].astype(F32) * cxp_ref[last_row:, :].astype(F32)
    nxt = gcn_ref[0:1, :].astype(F32) * cxn_ref[0:1, :].astype(F32)
    pos = i % per_seq
    prev = jnp.where(pos == 0, 0.0, prev)
    nxt = jnp.where(pos == per_seq - 1, 0.0, nxt)
    row = lax.broadcasted_iota(jnp.int32, (tm, 1), 0)
    u_dn = jnp.where(row == 0, prev, pltpu.roll(u, 1, 0))
    u_up = jnp.where(row == tm - 1, nxt, pltpu.roll(u, tm - 1, 0))
    w = w_ref[...]
    conv = gb_ref[...].astype(F32) * (u_dn * w[0:1] + u * w[1:2] + u_up * w[2:3])

    def norm(z, g):
        ms = jnp.mean(z * z, axis=-1, keepdims=True)
        return z * lax.rsqrt(ms + EPS) * g

    o_ref[:, :ATTN_WIDTH] = norm(attn_ref[...].astype(F32), ga_ref[...]).astype(o_ref.dtype)
    o_ref[:, ATTN_WIDTH:] = norm(conv, gv_ref[...]).astype(o_ref.dtype)


def _mix_norm(attn, proj, conv_w, ga, gv, seq, *, tm):
    t = attn.shape[0]
    per_seq = seq // tm
    halo = BF16_SUBLANES
    n_halo = t // halo
    cw = CONV_WIDTH

    def main(col):
        return pl.BlockSpec((tm, cw), lambda i: (i, col // cw))

    def prev(col):
        return pl.BlockSpec((halo, cw), lambda i: (jnp.maximum(i * (tm // halo) - 1, 0), col // cw))

    def nxt(col):
        return pl.BlockSpec((halo, cw), lambda i: (jnp.minimum((i + 1) * (tm // halo), n_halo - 1), col // cw))

    def vec(n):
        return pl.BlockSpec((n, cw), lambda i: (0, 0))

    return pl.pallas_call(
        functools.partial(_mix_kernel, per_seq=per_seq),
        grid=(t // tm,),
        in_specs=[pl.BlockSpec((tm, ATTN_WIDTH), lambda i: (i, 0)), main(COL_CX), main(COL_GB), main(COL_GC),
                  prev(COL_CX), prev(COL_GC), nxt(COL_CX), nxt(COL_GC), vec(3), vec(1), vec(1)],
        out_specs=pl.BlockSpec((tm, ATTN_WIDTH + cw), lambda i: (i, 0)),
        out_shape=jax.ShapeDtypeStruct((t, ATTN_WIDTH + cw), BF16),
        compiler_params=_params("parallel"),
        name="mix_norm",
    )(attn, proj, proj, proj, proj, proj, proj, proj, conv_w, ga.reshape(1, -1), gv.reshape(1, -1))


def _peer_scores_kernel(wq_ref, h_ref, sk_ref, a_ref, b_ref):
    qt = jnp.dot(wq_ref[...], h_ref[...], preferred_element_type=F32)

    def split(v):
        hi = v.astype(BF16)
        return hi, (v - hi.astype(F32)).astype(BF16)

    q_hi, q_lo = split(qt)
    for h in range(PEER_HEADS):
        for p, o_ref in enumerate((a_ref, b_ref)):
            rows = slice((2 * h + p) * N_KEYS, (2 * h + p + 1) * N_KEYS)
            k_hi, k_lo = split(sk_ref[h, p])
            lhs = jnp.concatenate([k_hi, k_hi, k_lo], axis=1)
            rhs = jnp.concatenate([q_hi[rows, :], q_lo[rows, :], q_hi[rows, :]], axis=0)
            o_ref[h * N_KEYS:(h + 1) * N_KEYS, :] = jnp.dot(lhs, rhs, preferred_element_type=F32)


def _peer_scores(wq_t, h_t, sub_keys, *, tn):
    rows, d = wq_t.shape
    t = h_t.shape[1]
    out = jax.ShapeDtypeStruct((PEER_HEADS * N_KEYS, t), F32)
    once = pl.Buffered(1)
    return pl.pallas_call(
        _peer_scores_kernel,
        grid=(t // tn,),
        in_specs=[
            pl.BlockSpec((rows, d), lambda i: (0, 0), pipeline_mode=once),
            pl.BlockSpec((d, tn), lambda i: (0, i)),
            pl.BlockSpec(sub_keys.shape, lambda i: (0, 0, 0, 0), pipeline_mode=once),
        ],
        out_specs=[pl.BlockSpec((PEER_HEADS * N_KEYS, tn), lambda i: (0, i))] * 2,
        out_shape=[out, out],
        compiler_params=_params("parallel"),
        name="peer_scores",
    )(wq_t, h_t, sub_keys)


def _top_sorted(work, n):
    out = []
    for _ in range(n):
        m = jnp.max(work, axis=0, keepdims=True)
        out.append(m)
        work = jnp.where(work == m, -jnp.inf, work)
    return out


SELECT_TOP_ROWS = 24
UP_CHUNKS = 4
K_CHUNKS = 8
GATE_ROWS = 16


def _peer_select_kernel(a_ref, b_ref, thr_ref, ea_ref, eb_ref, a_top, b_top):
    k = PEER_TOPK
    sub = 8
    a_top[...] = jnp.full(a_top.shape, -jnp.inf, F32)
    b_top[...] = jnp.full(b_top.shape, -jnp.inf, F32)
    for h in range(PEER_HEADS):
        rows = slice(h * N_KEYS, (h + 1) * N_KEYS)
        a = a_ref[rows, :]
        b = b_ref[rows, :]
        a_sorted = _top_sorted(a, k + 1)
        b_sorted = _top_sorted(b, k + 1)
        for r in range(k + 1):
            a_top[r:r + 1, :] = a_sorted[r]
            b_top[r:r + 1, :] = b_sorted[r]
        slabs = [a_sorted[0] + b_top[...]]
        for r in range(1, sub):
            slabs.append(a_sorted[r] + b_top[0:sub, :])
        slabs.append(a_top[sub:, :] + b_sorted[0])
        top = _top_sorted(jnp.concatenate(slabs, axis=0), k + 1)
        z = jnp.ones_like(top[0])
        for r in range(1, k):
            z = z + jnp.exp(top[r] - top[0])
        tau = 0.5 * (top[k - 1] + top[k])
        thr_ref[:, h, :] = tau - a
        ea_ref[:, h, :] = jnp.exp(a - a_sorted[0])
        eb_ref[rows, :] = jnp.exp(b - b_sorted[0]) / z


def _peer_select(a_t, b_t, *, tn):
    rows, t = a_t.shape
    spec = pl.BlockSpec((rows, tn), lambda i: (0, i))
    out = jax.ShapeDtypeStruct((rows, t), F32)
    key_spec = pl.BlockSpec((N_KEYS, PEER_HEADS, tn), lambda i: (0, 0, i))
    key_out = jax.ShapeDtypeStruct((N_KEYS, PEER_HEADS, t), F32)
    return pl.pallas_call(
        _peer_select_kernel,
        grid=(t // tn,),
        in_specs=[spec, spec],
        out_specs=[key_spec, key_spec, spec],
        out_shape=[key_out, key_out, out],
        scratch_shapes=[pltpu.VMEM((SELECT_TOP_ROWS, tn), F32), pltpu.VMEM((SELECT_TOP_ROWS, tn), F32)],
        compiler_params=_params("parallel"),
        name="peer_select",
    )(a_t, b_t)


def _peer_kernel(h_ref, wd_ref, wu_ref, thr_ref, ea_ref, b_ref, eb_ref, o_ref, s_buf0, s_buf1, a_buf0, a_buf1,
                 *, n_blocks, n_exp_blocks):
    s = pl.program_id(0)
    n_slabs = wd_ref.shape[0] // N_KEYS

    @pl.when(s == 0)
    def _():
        for buf in (s_buf0, s_buf1, a_buf0, a_buf1):
            buf[...] = jnp.zeros_like(buf)

    @pl.when((jnp.maximum(s - 2, 0) % n_exp_blocks) == 0)
    def _():
        o_ref[...] = jnp.zeros_like(o_ref)

    def step(s_cur, s_prv, a_cur, a_prv):
        first_key = (jnp.clip(s - 1, 0, n_blocks - 1) % n_exp_blocks) * n_slabs
        out_rows = o_ref.shape[0] // n_slabs
        tiles_per_slab = N_KEYS // GATE_ROWS
        k_chunk = h_ref.shape[0] // K_CHUNKS

        def gate_tile(ii, r, anchor):
            key = first_key + ii
            zero_bits = (lax.bitcast_convert_type(anchor, jnp.uint32) >> 16) >> 16
            g = lax.bitcast_convert_type(zero_bits, F32)
            for h in range(PEER_HEADS):
                sub = slice(h * N_KEYS + r * GATE_ROWS, h * N_KEYS + (r + 1) * GATE_ROWS)
                thr_row = thr_ref[key, pl.ds(h, 1), :]
                ea_row = ea_ref[key, pl.ds(h, 1), :]
                below = lax.bitcast_convert_type(thr_row - b_ref[sub, :], jnp.int32)
                keep = lax.bitcast_convert_type(eb_ref[sub, :], jnp.int32) & (below >> 31)
                g = g + lax.bitcast_convert_type(keep, F32) * ea_row
            srows = slice(ii * N_KEYS + r * GATE_ROWS, ii * N_KEYS + (r + 1) * GATE_ROWS)
            sv = s_prv[srows, :]
            act = 0.5 * sv * (1.0 + lax.erf(sv * INV_SQRT2))
            a_prv[srows, :] = (act * g).astype(a_prv.dtype)

        def up_piece(ii, u):
            up_rows = out_rows // UP_CHUNKS
            orow = slice(ii * out_rows + u * up_rows, ii * out_rows + (u + 1) * up_rows)
            upd = jnp.dot(wu_ref[orow, :], a_cur[...], preferred_element_type=F32)
            o_ref[orow, :] += upd
            return upd

        def down_piece(ii, q):
            pair = slice((ii // 2) * 2 * N_KEYS, ((ii // 2) * 2 + 2) * N_KEYS)
            kc = (ii % 2) * (K_CHUNKS // 2) + q
            ks = slice(kc * k_chunk, (kc + 1) * k_chunk)
            part = jnp.dot(wd_ref[pair, ks], h_ref[ks, :], preferred_element_type=F32)
            if kc == 0:
                s_cur[pair, :] = part
            else:
                s_cur[pair, :] += part
            return part

        for ii in range(n_slabs):
            pieces = []
            for j in range(max(UP_CHUNKS, K_CHUNKS // 2)):
                if j < UP_CHUNKS:
                    pieces.append(functools.partial(up_piece, ii, j))
                if j < K_CHUNKS // 2:
                    pieces.append(functools.partial(down_piece, ii, j))
            tiles = [[r for r in range(tiles_per_slab) if r * len(pieces) // tiles_per_slab == p]
                     for p in range(len(pieces))]
            for piece, mine in zip(pieces, tiles):
                res = piece()
                for n, r in enumerate(mine):
                    at = n * (res.shape[0] // len(mine))
                    gate_tile(ii, r, res[at:at + GATE_ROWS, :])

    pl.when(s % 2 == 0)(functools.partial(step, s_buf0, s_buf1, a_buf0, a_buf1))
    pl.when(s % 2 == 1)(functools.partial(step, s_buf1, s_buf0, a_buf1, a_buf0))


def _peer_mixer(h_t, wd, wu_blk, thr_t, ea_t, b_t, eb_t, *, tn):
    d, t = h_t.shape
    n_exp_blocks, _, te = wu_blk.shape
    n_blocks = (t // tn) * n_exp_blocks
    rows = b_t.shape[0]

    def blk(s, lag):
        return jnp.clip(s - lag, 0, n_blocks - 1)

    once = pl.Buffered(1)
    tok_spec = pl.BlockSpec((rows, tn), lambda s: (0, blk(s, 1) // n_exp_blocks), pipeline_mode=once)
    key_spec = pl.BlockSpec((N_KEYS, PEER_HEADS, tn), lambda s: (0, 0, blk(s, 1) // n_exp_blocks), pipeline_mode=once)
    return pl.pallas_call(
        functools.partial(_peer_kernel, n_blocks=n_blocks, n_exp_blocks=n_exp_blocks),
        grid=(n_blocks + 2,),
        in_specs=[
            pl.BlockSpec((d, tn), lambda s: (0, blk(s, 0) // n_exp_blocks), pipeline_mode=once),
            pl.BlockSpec((te, d), lambda s: (blk(s, 0) % n_exp_blocks, 0)),
            pl.BlockSpec((None, d, te), lambda s: (blk(s, 2) % n_exp_blocks, 0, 0)),
            key_spec, key_spec, tok_spec, tok_spec,
        ],
        out_specs=pl.BlockSpec((d, tn), lambda s: (0, blk(s, 2) // n_exp_blocks)),
        out_shape=jax.ShapeDtypeStruct((d, t), F32),
        scratch_shapes=[pltpu.VMEM((te, tn), F32), pltpu.VMEM((te, tn), F32),
                        pltpu.VMEM((te, tn), BF16), pltpu.VMEM((te, tn), BF16)],
        compiler_params=_params("arbitrary"),
        name="peer_mixer",
    )(h_t, wd, wu_blk, thr_t, ea_t, b_t, eb_t)


def _final_kernel(x_ref, p_ref, g_ref, o_ref):
    x = x_ref[...] + p_ref[...].T
    ms = jnp.mean(x * x, axis=-1, keepdims=True)
    o_ref[...] = x * lax.rsqrt(ms + EPS) * g_ref[...]


def _final(x1, peer_t, g, *, tm):
    t, d = x1.shape
    return pl.pallas_call(
        _final_kernel,
        grid=(t // tm,),
        in_specs=[pl.BlockSpec((tm, d), lambda i: (i, 0)), pl.BlockSpec((d, tm), lambda i: (0, i)),
                  pl.BlockSpec((1, d), lambda i: (0, 0))],
        out_specs=pl.BlockSpec((tm, d), lambda i: (i, 0)),
        out_shape=jax.ShapeDtypeStruct((t, d), F32),
        compiler_params=_params("parallel"),
        name="final_norm",
    )(x1, peer_t, g.reshape(1, d))


def _tiles(t, seq):
    return dict(
        norm_tm=min(256, t),
        mm_tm=min(1024, t),
        mm_tn=1024,
        qk_tm=min(256, seq),
        attn_tq=min(512, seq),
        attn_tk=min(1024, seq),
        mix_tm=min(256, seq),
        score_tn=min(512, t),
        select_tn=min(256, t),
        peer_tn=min(512, t),
        peer_te=512,
    )


def kernel(x, norm_mix_g, w_in, q_norm_g, k_norm_g, conv_w, attn_out_g, conv_out_g, w_out, norm_ffn_g,
           peer_w_query, peer_sub_keys, peer_w_down, peer_w_up, norm_final_g):
    batch, seq, d = x.shape
    assert w_in.shape[0] == 1, "one layer: the final norm is fused with the layer's PEER residual"
    t = batch * seq
    tl = _tiles(t, seq)
    xf = x.reshape(t, d)

    w_in_b = w_in[0].astype(BF16)
    w_out_b = w_out[0].astype(BF16)
    wq_t = _cast_transpose(peer_w_query[0], tm=tl["norm_tm"])
    wd_b = peer_w_down[0].astype(BF16)
    wu_blk = _cast_transpose(peer_w_up[0], tm=tl["peer_te"], blocked=True)

    h = _rmsnorm(xf, norm_mix_g[0], tm=tl["norm_tm"])
    n_cols, n_q, n_kv = IN_COLS // tl["mm_tn"], ATTN_WIDTH // tl["mm_tn"], 2 * KV_WIDTH // tl["mm_tn"]
    assert n_q * tl["mm_tn"] == ATTN_WIDTH and n_kv * tl["mm_tn"] == 2 * KV_WIDTH

    def w_in_block(j):
        return jnp.where(j < n_q, j, jnp.where(j < n_cols - n_kv, j + n_kv, j - (n_cols - n_kv) + n_q))

    proj = _matmul(h, w_in_b, tm=tl["mm_tm"], tn=tl["mm_tn"], out_dtype=BF16, col_block=w_in_block, name="in_proj")
    q, k = _qk_prep(proj, q_norm_g[0], k_norm_g[0], seq, tm=tl["qk_tm"])
    attn = _attention(q, k, proj, batch, seq, tq=tl["attn_tq"], tk=tl["attn_tk"])
    y = _mix_norm(attn, proj, conv_w[0], attn_out_g[0], conv_out_g[0], seq, tm=tl["mix_tm"])
    x1 = _matmul(y, w_out_b, tm=tl["mm_tm"], tn=tl["mm_tn"], out_dtype=F32, residual=xf, name="out_proj")

    h2_t = _rmsnorm(x1, norm_ffn_g[0], tm=tl["norm_tm"], transpose=True)
    a_t, b_t = _peer_scores(wq_t, h2_t, peer_sub_keys[0], tn=tl["score_tn"])
    thr_t, ea_t, eb_t = _peer_select(a_t, b_t, tn=tl["select_tn"])
    peer_t = _peer_mixer(h2_t, wd_b, wu_blk, thr_t, ea_t, b_t, eb_t, tn=tl["peer_tn"])
    out = _final(x1, peer_t, norm_final_g, tm=tl["norm_tm"])
    return out.reshape(batch, seq, d)
```

```python
import functools

import numpy as np
import jax
import jax.numpy as jnp
from jax import lax
from jax.experimental import pallas as pl
from jax.experimental.pallas import tpu as pltpu

F32 = jnp.float32
BF16 = jnp.bfloat16

HEAD_DIM = 128
N_Q_HEADS = 16
N_KV_HEADS = 4
GQA_GROUP = N_Q_HEADS // N_KV_HEADS
ATTN_WIDTH = N_Q_HEADS * HEAD_DIM
KV_WIDTH = N_KV_HEADS * HEAD_DIM
CONV_WIDTH = 2048
GRID_W = 64
ROPE_THETA = 10000.0
N_KEYS = 128
PEER_HEADS = 8
PEER_TOPK = 16
EPS = 1e-6
INV_SQRT2 = 0.7071067811865476
LOG2_E = 1.4426950408889634
Q_SCALE = HEAD_DIM ** -0.5 * LOG2_E

LANES = 128
BF16_SUBLANES = 16
VMEM_LIMIT_BYTES = 56 * 1024 * 1024

COL_Q = 0
COL_CX = ATTN_WIDTH
COL_GB = COL_CX + CONV_WIDTH
COL_GC = COL_GB + CONV_WIDTH
COL_K = COL_GC + CONV_WIDTH
COL_V = COL_K + KV_WIDTH
IN_COLS = COL_V + KV_WIDTH


def _params(*sem):
    return pltpu.CompilerParams(dimension_semantics=sem, vmem_limit_bytes=VMEM_LIMIT_BYTES)


def _rmsnorm_kernel(x_ref, g_ref, o_ref, *, transpose):
    x = x_ref[...]
    ms = jnp.mean(x * x, axis=-1, keepdims=True)
    y = x * lax.rsqrt(ms + EPS) * g_ref[...]
    if transpose:
        y = y.T
    o_ref[...] = y.astype(o_ref.dtype)


def _rmsnorm(x, g, *, tm, transpose=False):
    t, d = x.shape
    if transpose:
        out_shape = jax.ShapeDtypeStruct((d, t), BF16)
        out_spec = pl.BlockSpec((d, tm), lambda i: (0, i))
    else:
        out_shape = jax.ShapeDtypeStruct((t, d), BF16)
        out_spec = pl.BlockSpec((tm, d), lambda i: (i, 0))
    return pl.pallas_call(
        functools.partial(_rmsnorm_kernel, transpose=transpose),
        grid=(t // tm,),
        in_specs=[pl.BlockSpec((tm, d), lambda i: (i, 0)), pl.BlockSpec((1, d), lambda i: (0, 0))],
        out_specs=out_spec,
        out_shape=out_shape,
        compiler_params=_params("parallel"),
        name="rmsnorm_t" if transpose else "rmsnorm",
    )(x, g.reshape(1, d))


def _cast_t_kernel(w_ref, o_ref):
    o_ref[...] = w_ref[...].T.astype(o_ref.dtype).reshape(o_ref.shape)


def _cast_transpose(w, *, tm, blocked=False):
    rows, cols = w.shape
    if blocked:
        out_spec = pl.BlockSpec((1, cols, tm), lambda i: (i, 0, 0))
        out_shape = jax.ShapeDtypeStruct((rows // tm, cols, tm), BF16)
    else:
        out_spec = pl.BlockSpec((cols, tm), lambda i: (0, i))
        out_shape = jax.ShapeDtypeStruct((cols, rows), BF16)
    return pl.pallas_call(
        _cast_t_kernel,
        grid=(rows // tm,),
        in_specs=[pl.BlockSpec((tm, cols), lambda i: (i, 0))],
        out_specs=out_spec,
        out_shape=out_shape,
        compiler_params=_params("parallel"),
        name="cast_transpose",
    )(w)


def _mm_kernel(a_ref, b_ref, o_ref):
    o_ref[...] = jnp.dot(a_ref[...], b_ref[...], preferred_element_type=F32).astype(o_ref.dtype)


def _mm_res_kernel(a_ref, b_ref, r_ref, o_ref):
    acc = jnp.dot(a_ref[...], b_ref[...], preferred_element_type=F32)
    o_ref[...] = (r_ref[...] + acc).astype(o_ref.dtype)


def _matmul(a, b, *, tm, tn, out_dtype, residual=None, col_block=lambda j: j, name):
    m, k = a.shape
    _, n = b.shape
    in_specs = [pl.BlockSpec((tm, k), lambda i, j: (i, 0)), pl.BlockSpec((k, tn), lambda i, j: (0, col_block(j)))]
    args = [a, b]
    body = _mm_kernel
    if residual is not None:
        in_specs.append(pl.BlockSpec((tm, tn), lambda i, j: (i, j)))
        args.append(residual)
        body = _mm_res_kernel
    return pl.pallas_call(
        body,
        grid=(m // tm, n // tn),
        in_specs=in_specs,
        out_specs=pl.BlockSpec((tm, tn), lambda i, j: (i, j)),
        out_shape=jax.ShapeDtypeStruct((m, n), out_dtype),
        compiler_params=_params("parallel", "parallel"),
        name=name,
    )(*args)


def _rope_tables(seq):
    t = np.arange(seq)
    half = HEAD_DIM // 2
    inv = ROPE_THETA ** (-np.arange(0, half, 2, dtype=np.float64) / half)
    ang_row = (t // GRID_W)[:, None] * inv
    ang_col = (t % GRID_W)[:, None] * inv
    zeros = np.zeros_like(ang_row)
    cos = np.concatenate([np.cos(ang_row)] * 2 + [np.cos(ang_col)] * 2, axis=1)
    sa = np.concatenate([-np.sin(ang_row), zeros, -np.sin(ang_col), zeros], axis=1)
    sb = np.concatenate([zeros, np.sin(ang_row), zeros, np.sin(ang_col)], axis=1)
    return (jnp.asarray(cos, F32), jnp.asarray(sa, F32), jnp.asarray(sb, F32))


def _qk_kernel(q_ref, k_ref, c_ref, sa_ref, sb_ref, gq_ref, gk_ref, qo_ref, ko_ref):
    c, sa, sb = c_ref[...], sa_ref[...], sb_ref[...]
    quarter = HEAD_DIM // 4

    def prep(x, g, scale):
        ms = jnp.mean(x * x, axis=-1, keepdims=True)
        y = x * lax.rsqrt(ms + EPS) * g
        r = y * c + pltpu.roll(y, HEAD_DIM - quarter, 1) * sa + pltpu.roll(y, quarter, 1) * sb
        return r * scale

    for h in range(N_Q_HEADS):
        sl = slice(h * HEAD_DIM, (h + 1) * HEAD_DIM)
        qo_ref[:, sl] = prep(q_ref[:, sl].astype(F32), gq_ref[...], Q_SCALE).astype(qo_ref.dtype)
    for h in range(N_KV_HEADS):
        sl = slice(h * HEAD_DIM, (h + 1) * HEAD_DIM)
        ko_ref[:, sl] = prep(k_ref[:, sl].astype(F32), gk_ref[...], 1.0).astype(ko_ref.dtype)


def _qk_prep(proj, gq, gk, seq, *, tm):
    t = proj.shape[0]
    tables = _rope_tables(seq)
    per_seq = seq // tm
    tab_spec = pl.BlockSpec((tm, HEAD_DIM), lambda i: (i % per_seq, 0))
    g_spec = pl.BlockSpec((1, HEAD_DIM), lambda i: (0, 0))
    return pl.pallas_call(
        _qk_kernel,
        grid=(t // tm,),
        in_specs=[
            pl.BlockSpec((tm, ATTN_WIDTH), lambda i: (i, COL_Q // ATTN_WIDTH)),
            pl.BlockSpec((tm, KV_WIDTH), lambda i: (i, COL_K // KV_WIDTH)),
            tab_spec, tab_spec, tab_spec, g_spec, g_spec,
        ],
        out_specs=[pl.BlockSpec((tm, ATTN_WIDTH), lambda i: (i, 0)), pl.BlockSpec((tm, KV_WIDTH), lambda i: (i, 0))],
        out_shape=[jax.ShapeDtypeStruct((t, ATTN_WIDTH), BF16), jax.ShapeDtypeStruct((t, KV_WIDTH), BF16)],
        compiler_params=_params("parallel"),
        name="qk_prep",
    )(proj, proj, *tables, gq.reshape(1, HEAD_DIM), gk.reshape(1, HEAD_DIM))


def _attn_kernel(q_ref, k_ref, v_ref, o_ref, *, tk):
    tq = q_ref.shape[0]
    n_chunks = k_ref.shape[0] // tk
    ones = jnp.ones((tk, HEAD_DIM), v_ref.dtype)
    for h in range(GQA_GROUP):
        sl = slice(h * HEAD_DIM, (h + 1) * HEAD_DIM)
        q = q_ref[:, sl]

        def body(i, carry, q=q):
            m, acc = carry
            off = pl.multiple_of(i * tk, tk)
            kc = k_ref[pl.ds(off, tk), :]
            vc = jnp.concatenate([v_ref[pl.ds(off, tk), :], ones], axis=1)
            s = lax.dot_general(q, kc, (((1,), (1,)), ((), ())), preferred_element_type=F32)
            m_new = jnp.maximum(m, jnp.max(s, axis=-1, keepdims=True))
            alpha = jnp.exp2(m - m_new)
            p = jnp.exp2(s - m_new)
            acc = alpha * acc + jnp.dot(p.astype(BF16), vc, preferred_element_type=F32)
            return m_new, acc

        init = (jnp.full((tq, 1), -jnp.inf, F32), jnp.zeros((tq, 2 * HEAD_DIM), F32))
        _, acc = lax.fori_loop(0, n_chunks, body, init, unroll=True)
        o_ref[:, sl] = (acc[:, :HEAD_DIM] / acc[:, HEAD_DIM:]).astype(o_ref.dtype)


def _attention(q, k, proj, batch, seq, *, tq, tk):
    t = q.shape[0]
    nq = seq // tq
    group_w = GQA_GROUP * HEAD_DIM
    return pl.pallas_call(
        functools.partial(_attn_kernel, tk=tk),
        grid=(batch, N_KV_HEADS, nq),
        in_specs=[
            pl.BlockSpec((tq, group_w), lambda b, g, i: (b * nq + i, g)),
            pl.BlockSpec((seq, HEAD_DIM), lambda b, g, i: (b, g)),
            pl.BlockSpec((seq, HEAD_DIM), lambda b, g, i: (b, COL_V // HEAD_DIM + g)),
        ],
        out_specs=pl.BlockSpec((tq, group_w), lambda b, g, i: (b * nq + i, g)),
        out_shape=jax.ShapeDtypeStruct((t, ATTN_WIDTH), BF16),
        compiler_params=_params("parallel", "parallel", "parallel"),
        name="attention",
    )(q, k, proj)


def _mix_kernel(attn_ref, cx_ref, gb_ref, gc_ref, cxp_ref, gcp_ref, cxn_ref, gcn_ref, w_ref, ga_ref, gv_ref,
                o_ref, *, per_seq):
    i = pl.program_id(0)
    tm = cx_ref.shape[0]
    u = gc_ref[...].astype(F32) * cx_ref[...].astype(F32)
    last_row = BF16_SUBLANES - 1
    prev = gcp_ref[last_row:, :].astype(F32) * cxp_ref[last_row:, :].astype(F32)
    nxt = gcn_ref[0:1, :].astype(F32) * cxn_ref[0:1, :].astype(F32)
    pos = i % per_seq
    prev = jnp.where(pos == 0, 0.0, prev)
    nxt = jnp.where(pos == per_seq - 1, 0.0, nxt)
    row = lax.broadcasted_iota(jnp.int32, (tm, 1), 0)
    u_dn = jnp.where(row == 0, prev, pltpu.roll(u, 1, 0))
    u_up = jnp.where(row == tm - 1, nxt, pltpu.roll(u, tm - 1, 0))
    w = w_ref[...]
    conv = gb_ref[...].astype(F32) * (u_dn * w[0:1] + u * w[1:2] + u_up * w[2:3])

    def norm(z, g):
        ms = jnp.mean(z * z, axis=-1, keepdims=True)
        return z * lax.rsqrt(ms + EPS) * g

    o_ref[:, :ATTN_WIDTH] = norm(attn_ref[...].astype(F32), ga_ref[...]).astype(o_ref.dtype)
    o_ref[:, ATTN_WIDTH:] = norm(conv, gv_ref[...]).astype(o_ref.dtype)


def _mix_norm(attn, proj, conv_w, ga, gv, seq, *, tm):
    t = attn.shape[0]
    per_seq = seq // tm
    halo = BF16_SUBLANES
    n_halo = t // halo
    cw = CONV_WIDTH

    def main(col):
        return pl.BlockSpec((tm, cw), lambda i: (i, col // cw))

    def prev(col):
        return pl.BlockSpec((halo, cw), lambda i: (jnp.maximum(i * (tm // halo) - 1, 0), col // cw))

    def nxt(col):
        return pl.BlockSpec((halo, cw), lambda i: (jnp.minimum((i + 1) * (tm // halo), n_halo - 1), col // cw))

    def vec(n):
        return pl.BlockSpec((n, cw), lambda i: (0, 0))

    return pl.pallas_call(
        functools.partial(_mix_kernel, per_seq=per_seq),
        grid=(t // tm,),
        in_specs=[pl.BlockSpec((tm, ATTN_WIDTH), lambda i: (i, 0)), main(COL_CX), main(COL_GB), main(COL_GC),
                  prev(COL_CX), prev(COL_GC), nxt(COL_CX), nxt(COL_GC), vec(3), vec(1), vec(1)],
        out_specs=pl.BlockSpec((tm, ATTN_WIDTH + cw), lambda i: (i, 0)),
        out_shape=jax.ShapeDtypeStruct((t, ATTN_WIDTH + cw), BF16),
        compiler_params=_params("parallel"),
        name="mix_norm",
    )(attn, proj, proj, proj, proj, proj, proj, proj, conv_w, ga.reshape(1, -1), gv.reshape(1, -1))


def _peer_scores_kernel(wq_ref, h_ref, sk_ref, a_ref, b_ref):
    qt = jnp.dot(wq_ref[...], h_ref[...], preferred_element_type=F32)

    def split(v):
        hi = v.astype(BF16)
        return hi, (v - hi.astype(F32)).astype(BF16)

    q_hi, q_lo = split(qt)
    for h in range(PEER_HEADS):
        for p, o_ref in enumerate((a_ref, b_ref)):
            rows = slice((2 * h + p) * N_KEYS, (2 * h + p + 1) * N_KEYS)
            k_hi, k_lo = split(sk_ref[h, p])
            lhs = jnp.concatenate([k_hi, k_hi, k_lo], axis=1)
            rhs = jnp.concatenate([q_hi[rows, :], q_lo[rows, :], q_hi[rows, :]], axis=0)
            o_ref[h * N_KEYS:(h + 1) * N_KEYS, :] = jnp.dot(lhs, rhs, preferred_element_type=F32)


def _peer_scores(wq_t, h_t, sub_keys, *, tn):
    rows, d = wq_t.shape
    t = h_t.shape[1]
    out = jax.ShapeDtypeStruct((PEER_HEADS * N_KEYS, t), F32)
    once = pl.Buffered(1)
    return pl.pallas_call(
        _peer_scores_kernel,
        grid=(t // tn,),
        in_specs=[
            pl.BlockSpec((rows, d), lambda i: (0, 0), pipeline_mode=once),
            pl.BlockSpec((d, tn), lambda i: (0, i)),
            pl.BlockSpec(sub_keys.shape, lambda i: (0, 0, 0, 0), pipeline_mode=once),
        ],
        out_specs=[pl.BlockSpec((PEER_HEADS * N_KEYS, tn), lambda i: (0, i))] * 2,
        out_shape=[out, out],
        compiler_params=_params("parallel"),
        name="peer_scores",
    )(wq_t, h_t, sub_keys)


def _top_sorted(work, n):
    out = []
    for _ in range(n):
        m = jnp.max(work, axis=0, keepdims=True)
        out.append(m)
        work = jnp.where(work == m, -jnp.inf, work)
    return out


SELECT_TOP_ROWS = 24
UP_CHUNKS = 4
K_CHUNKS = 8
GATE_ROWS = 16


def _peer_select_kernel(a_ref, b_ref, thr_ref, ea_ref, eb_ref, a_top, b_top):
    k = PEER_TOPK
    sub = 8
    a_top[...] = jnp.full(a_top.shape, -jnp.inf, F32)
    b_top[...] = jnp.full(b_top.shape, -jnp.inf, F32)
    for h in range(PEER_HEADS):
        rows = slice(h * N_KEYS, (h + 1) * N_KEYS)
        a = a_ref[rows, :]
        b = b_ref[rows, :]
        a_sorted = _top_sorted(a, k + 1)
        b_sorted = _top_sorted(b, k + 1)
        for r in range(k + 1):
            a_top[r:r + 1, :] = a_sorted[r]
            b_top[r:r + 1, :] = b_sorted[r]
        slabs = [a_sorted[0] + b_top[...]]
        for r in range(1, sub):
            slabs.append(a_sorted[r] + b_top[0:sub, :])
        slabs.append(a_top[sub:, :] + b_sorted[0])
        top = _top_sorted(jnp.concatenate(slabs, axis=0), k + 1)
        z = jnp.ones_like(top[0])
        for r in range(1, k):
            z = z + jnp.exp(top[r] - top[0])
        tau = 0.5 * (top[k - 1] + top[k])
        thr_ref[:, h, :] = tau - a
        ea_ref[:, h, :] = jnp.exp(a - a_sorted[0])
        eb_ref[rows, :] = jnp.exp(b - b_sorted[0]) / z


def _peer_select(a_t, b_t, *, tn):
    rows, t = a_t.shape
    spec = pl.BlockSpec((rows, tn), lambda i: (0, i))
    out = jax.ShapeDtypeStruct((rows, t), F32)
    key_spec = pl.BlockSpec((N_KEYS, PEER_HEADS, tn), lambda i: (0, 0, i))
    key_out = jax.ShapeDtypeStruct((N_KEYS, PEER_HEADS, t), F32)
    return pl.pallas_call(
        _peer_select_kernel,
        grid=(t // tn,),
        in_specs=[spec, spec],
        out_specs=[key_spec, key_spec, spec],
        out_shape=[key_out, key_out, out],
        scratch_shapes=[pltpu.VMEM((SELECT_TOP_ROWS, tn), F32), pltpu.VMEM((SELECT_TOP_ROWS, tn), F32)],
        compiler_params=_params("parallel"),
        name="peer_select",
    )(a_t, b_t)


def _peer_kernel(h_ref, wd_ref, wu_ref, thr_ref, ea_ref, b_ref, eb_ref, o_ref, s_buf0, s_buf1, a_buf0, a_buf1,
                 *, n_blocks, n_exp_blocks):
    s = pl.program_id(0)
    n_slabs = wd_ref.shape[0] // N_KEYS

    @pl.when(s == 0)
    def _():
        for buf in (s_buf0, s_buf1, a_buf0, a_buf1):
            buf[...] = jnp.zeros_like(buf)

    @pl.when((jnp.maximum(s - 2, 0) % n_exp_blocks) == 0)
    def _():
        o_ref[...] = jnp.zeros_like(o_ref)

    def step(s_cur, s_prv, a_cur, a_prv):
        first_key = (jnp.clip(s - 1, 0, n_blocks - 1) % n_exp_blocks) * n_slabs
        out_rows = o_ref.shape[0] // n_slabs
        tiles_per_slab = N_KEYS // GATE_ROWS
        k_chunk = h_ref.shape[0] // K_CHUNKS

        def gate_tile(ii, r, anchor):
            key = first_key + ii
            zero_bits = (lax.bitcast_convert_type(anchor, jnp.uint32) >> 16) >> 16
            g = lax.bitcast_convert_type(zero_bits, F32)
            for h in range(PEER_HEADS):
                sub = slice(h * N_KEYS + r * GATE_ROWS, h * N_KEYS + (r + 1) * GATE_ROWS)
                thr_row = thr_ref[key, pl.ds(h, 1), :]
                ea_row = ea_ref[key, pl.ds(h, 1), :]
                below = lax.bitcast_convert_type(thr_row - b_ref[sub, :], jnp.int32)
                keep = lax.bitcast_convert_type(eb_ref[sub, :], jnp.int32) & (below >> 31)
                g = g + lax.bitcast_convert_type(keep, F32) * ea_row
            srows = slice(ii * N_KEYS + r * GATE_ROWS, ii * N_KEYS + (r + 1) * GATE_ROWS)
            sv = s_prv[srows, :]
            act = 0.5 * sv * (1.0 + lax.erf(sv * INV_SQRT2))
            a_prv[srows, :] = (act * g).astype(a_prv.dtype)

        def up_piece(ii, u):
            up_rows = out_rows // UP_CHUNKS
            orow = slice(ii * out_rows + u * up_rows, ii * out_rows + (u + 1) * up_rows)
            upd = jnp.dot(wu_ref[orow, :], a_cur[...], preferred_element_type=F32)
            o_ref[orow, :] += upd
            return upd

        def down_piece(ii, q):
            pair = slice((ii // 2) * 2 * N_KEYS, ((ii // 2) * 2 + 2) * N_KEYS)
            kc = (ii % 2) * (K_CHUNKS // 2) + q
            ks = slice(kc * k_chunk, (kc + 1) * k_chunk)
            part = jnp.dot(wd_ref[pair, ks], h_ref[ks, :], preferred_element_type=F32)
            if kc == 0:
                s_cur[pair, :] = part
            else:
                s_cur[pair, :] += part
            return part

        for ii in range(n_slabs):
            pieces = []
            for j in range(max(UP_CHUNKS, K_CHUNKS // 2)):
                if j < UP_CHUNKS:
                    pieces.append(functools.partial(up_piece, ii, j))
                if j < K_CHUNKS // 2:
                    pieces.append(functools.partial(down_piece, ii, j))
            tiles = [[r for r in range(tiles_per_slab) if r * len(pieces) // tiles_per_slab == p]
                     for p in range(len(pieces))]
            for piece, mine in zip(pieces, tiles):
                res = piece()
                for n, r in enumerate(mine):
                    at = n * (res.shape[0] // len(mine))
                    gate_tile(ii, r, res[at:at + GATE_ROWS, :])

    pl.when(s % 2 == 0)(functools.partial(step, s_buf0, s_buf1, a_buf0, a_buf1))
    pl.when(s % 2 == 1)(functools.partial(step, s_buf1, s_buf0, a_buf1, a_buf0))


def _peer_mixer(h_t, wd, wu_blk, thr_t, ea_t, b_t, eb_t, *, tn):
    d, t = h_t.shape
    n_exp_blocks, _, te = wu_blk.shape
    n_blocks = (t // tn) * n_exp_blocks
    rows = b_t.shape[0]

    def blk(s, lag):
        return jnp.clip(s - lag, 0, n_blocks - 1)

    once = pl.Buffered(1)
    tok_spec = pl.BlockSpec((rows, tn), lambda s: (0, blk(s, 1) // n_exp_blocks), pipeline_mode=once)
    key_spec = pl.BlockSpec((N_KEYS, PEER_HEADS, tn), lambda s: (0, 0, blk(s, 1) // n_exp_blocks))
    return pl.pallas_call(
        functools.partial(_peer_kernel, n_blocks=n_blocks, n_exp_blocks=n_exp_blocks),
        grid=(n_blocks + 2,),
        in_specs=[
            pl.BlockSpec((d, tn), lambda s: (0, blk(s, 0) // n_exp_blocks)),
            pl.BlockSpec((te, d), lambda s: (blk(s, 0) % n_exp_blocks, 0)),
            pl.BlockSpec((None, d, te), lambda s: (blk(s, 2) % n_exp_blocks, 0, 0)),
            key_spec, key_spec, tok_spec, tok_spec,
        ],
        out_specs=pl.BlockSpec((d, tn), lambda s: (0, blk(s, 2) // n_exp_blocks)),
        out_shape=jax.ShapeDtypeStruct((d, t), F32),
        scratch_shapes=[pltpu.VMEM((te, tn), F32), pltpu.VMEM((te, tn), F32),
                        pltpu.VMEM((te, tn), BF16), pltpu.VMEM((te, tn), BF16)],
        compiler_params=_params("arbitrary"),
        name="peer_mixer",
    )(h_t, wd, wu_blk, thr_t, ea_t, b_t, eb_t)


def _final_kernel(x_ref, p_ref, g_ref, o_ref):
    x = x_ref[...] + p_ref[...].T
    ms = jnp.mean(x * x, axis=-1, keepdims=True)
    o_ref[...] = x * lax.rsqrt(ms + EPS) * g_ref[...]


def _final(x1, peer_t, g, *, tm):
    t, d = x1.shape
    return pl.pallas_call(
        _final_kernel,
        grid=(t // tm,),
        in_specs=[pl.BlockSpec((tm, d), lambda i: (i, 0)), pl.BlockSpec((d, tm), lambda i: (0, i)),
                  pl.BlockSpec((1, d), lambda i: (0, 0))],
        out_specs=pl.BlockSpec((tm, d), lambda i: (i, 0)),
        out_shape=jax.ShapeDtypeStruct((t, d), F32),
        compiler_params=_params("parallel"),
        name="final_norm",
    )(x1, peer_t, g.reshape(1, d))


def _tiles(t, seq):
    return dict(
        norm_tm=min(256, t),
        mm_tm=min(1024, t),
        mm_tn=1024,
        qk_tm=min(256, seq),
        attn_tq=min(512, seq),
        attn_tk=min(1024, seq),
        mix_tm=min(256, seq),
        score_tn=min(512, t),
        select_tn=min(256, t),
        peer_tn=min(512, t),
        peer_te=512,
    )


def kernel(x, norm_mix_g, w_in, q_norm_g, k_norm_g, conv_w, attn_out_g, conv_out_g, w_out, norm_ffn_g,
           peer_w_query, peer_sub_keys, peer_w_down, peer_w_up, norm_final_g):
    batch, seq, d = x.shape
    assert w_in.shape[0] == 1, "one layer: the final norm is fused with the layer's PEER residual"
    t = batch * seq
    tl = _tiles(t, seq)
    xf = x.reshape(t, d)

    w_in_b = w_in[0].astype(BF16)
    w_out_b = w_out[0].astype(BF16)
    wq_t = _cast_transpose(peer_w_query[0], tm=tl["norm_tm"])
    wd_b = peer_w_down[0].astype(BF16)
    wu_blk = _cast_transpose(peer_w_up[0], tm=tl["peer_te"], blocked=True)

    h = _rmsnorm(xf, norm_mix_g[0], tm=tl["norm_tm"])
    n_cols, n_q, n_kv = IN_COLS // tl["mm_tn"], ATTN_WIDTH // tl["mm_tn"], 2 * KV_WIDTH // tl["mm_tn"]
    assert n_q * tl["mm_tn"] == ATTN_WIDTH and n_kv * tl["mm_tn"] == 2 * KV_WIDTH

    def w_in_block(j):
        return jnp.where(j < n_q, j, jnp.where(j < n_cols - n_kv, j + n_kv, j - (n_cols - n_kv) + n_q))

    proj = _matmul(h, w_in_b, tm=tl["mm_tm"], tn=tl["mm_tn"], out_dtype=BF16, col_block=w_in_block, name="in_proj")
    q, k = _qk_prep(proj, q_norm_g[0], k_norm_g[0], seq, tm=tl["qk_tm"])
    attn = _attention(q, k, proj, batch, seq, tq=tl["attn_tq"], tk=tl["attn_tk"])
    y = _mix_norm(attn, proj, conv_w[0], attn_out_g[0], conv_out_g[0], seq, tm=tl["mix_tm"])
    x1 = _matmul(y, w_out_b, tm=tl["mm_tm"], tn=tl["mm_tn"], out_dtype=F32, residual=xf, name="out_proj")

    h2_t = _rmsnorm(x1, norm_ffn_g[0], tm=tl["norm_tm"], transpose=True)
    a_t, b_t = _peer_scores(wq_t, h2_t, peer_sub_keys[0], tn=tl["score_tn"])
    thr_t, ea_t, eb_t = _peer_select(a_t, b_t, tn=tl["select_tn"])
    peer_t = _peer_mixer(h2_t, wd_b, wu_blk, thr_t, ea_t, b_t, eb_t, tn=tl["peer_tn"])
    out = _final(x1, peer_t, norm_final_g, tm=tl["norm_tm"])
    return out.reshape(batch, seq, d)
```

```python
import functools

import numpy as np
import jax
import jax.numpy as jnp
from jax import lax
from jax.experimental import pallas as pl
from jax.experimental.pallas import tpu as pltpu

F32 = jnp.float32
BF16 = jnp.bfloat16

HEAD_DIM = 128
N_Q_HEADS = 16
N_KV_HEADS = 4
GQA_GROUP = N_Q_HEADS // N_KV_HEADS
ATTN_WIDTH = N_Q_HEADS * HEAD_DIM
KV_WIDTH = N_KV_HEADS * HEAD_DIM
CONV_WIDTH = 2048
GRID_W = 64
ROPE_THETA = 10000.0
N_KEYS = 128
PEER_HEADS = 8
PEER_TOPK = 16
EPS = 1e-6
INV_SQRT2 = 0.7071067811865476
LOG2_E = 1.4426950408889634
Q_SCALE = HEAD_DIM ** -0.5 * LOG2_E

LANES = 128
BF16_SUBLANES = 16
VMEM_LIMIT_BYTES = 56 * 1024 * 1024

COL_Q = 0
COL_CX = ATTN_WIDTH
COL_GB = COL_CX + CONV_WIDTH
COL_GC = COL_GB + CONV_WIDTH
COL_K = COL_GC + CONV_WIDTH
COL_V = COL_K + KV_WIDTH
IN_COLS = COL_V + KV_WIDTH


def _params(*sem):
    return pltpu.CompilerParams(dimension_semantics=sem, vmem_limit_bytes=VMEM_LIMIT_BYTES)


def _rmsnorm_kernel(x_ref, g_ref, o_ref, *, transpose):
    x = x_ref[...]
    ms = jnp.mean(x * x, axis=-1, keepdims=True)
    y = x * lax.rsqrt(ms + EPS) * g_ref[...]
    if transpose:
        y = y.T
    o_ref[...] = y.astype(o_ref.dtype)


def _rmsnorm(x, g, *, tm, transpose=False):
    t, d = x.shape
    if transpose:
        out_shape = jax.ShapeDtypeStruct((d, t), BF16)
        out_spec = pl.BlockSpec((d, tm), lambda i: (0, i))
    else:
        out_shape = jax.ShapeDtypeStruct((t, d), BF16)
        out_spec = pl.BlockSpec((tm, d), lambda i: (i, 0))
    return pl.pallas_call(
        functools.partial(_rmsnorm_kernel, transpose=transpose),
        grid=(t // tm,),
        in_specs=[pl.BlockSpec((tm, d), lambda i: (i, 0)), pl.BlockSpec((1, d), lambda i: (0, 0))],
        out_specs=out_spec,
        out_shape=out_shape,
        compiler_params=_params("parallel"),
        name="rmsnorm_t" if transpose else "rmsnorm",
    )(x, g.reshape(1, d))


def _cast_t_kernel(w_ref, o_ref):
    o_ref[...] = w_ref[...].T.astype(o_ref.dtype).reshape(o_ref.shape)


def _cast_transpose(w, *, tm, blocked=False):
    rows, cols = w.shape
    if blocked:
        out_spec = pl.BlockSpec((1, cols, tm), lambda i: (i, 0, 0))
        out_shape = jax.ShapeDtypeStruct((rows // tm, cols, tm), BF16)
    else:
        out_spec = pl.BlockSpec((cols, tm), lambda i: (0, i))
        out_shape = jax.ShapeDtypeStruct((cols, rows), BF16)
    return pl.pallas_call(
        _cast_t_kernel,
        grid=(rows // tm,),
        in_specs=[pl.BlockSpec((tm, cols), lambda i: (i, 0))],
        out_specs=out_spec,
        out_shape=out_shape,
        compiler_params=_params("parallel"),
        name="cast_transpose",
    )(w)


def _mm_kernel(a_ref, b_ref, o_ref):
    o_ref[...] = jnp.dot(a_ref[...], b_ref[...], preferred_element_type=F32).astype(o_ref.dtype)


def _mm_res_kernel(a_ref, b_ref, r_ref, o_ref):
    acc = jnp.dot(a_ref[...], b_ref[...], preferred_element_type=F32)
    o_ref[...] = (r_ref[...] + acc).astype(o_ref.dtype)


def _matmul(a, b, *, tm, tn, out_dtype, residual=None, col_block=lambda j: j, name):
    m, k = a.shape
    _, n = b.shape
    in_specs = [pl.BlockSpec((tm, k), lambda i, j: (i, 0)), pl.BlockSpec((k, tn), lambda i, j: (0, col_block(j)))]
    args = [a, b]
    body = _mm_kernel
    if residual is not None:
        in_specs.append(pl.BlockSpec((tm, tn), lambda i, j: (i, j)))
        args.append(residual)
        body = _mm_res_kernel
    return pl.pallas_call(
        body,
        grid=(m // tm, n // tn),
        in_specs=in_specs,
        out_specs=pl.BlockSpec((tm, tn), lambda i, j: (i, j)),
        out_shape=jax.ShapeDtypeStruct((m, n), out_dtype),
        compiler_params=_params("parallel", "parallel"),
        name=name,
    )(*args)


def _rope_tables(seq):
    t = np.arange(seq)
    half = HEAD_DIM // 2
    inv = ROPE_THETA ** (-np.arange(0, half, 2, dtype=np.float64) / half)
    ang_row = (t // GRID_W)[:, None] * inv
    ang_col = (t % GRID_W)[:, None] * inv
    zeros = np.zeros_like(ang_row)
    cos = np.concatenate([np.cos(ang_row)] * 2 + [np.cos(ang_col)] * 2, axis=1)
    sa = np.concatenate([-np.sin(ang_row), zeros, -np.sin(ang_col), zeros], axis=1)
    sb = np.concatenate([zeros, np.sin(ang_row), zeros, np.sin(ang_col)], axis=1)
    return (jnp.asarray(cos, F32), jnp.asarray(sa, F32), jnp.asarray(sb, F32))


def _qk_kernel(q_ref, k_ref, c_ref, sa_ref, sb_ref, gq_ref, gk_ref, qo_ref, ko_ref):
    c, sa, sb = c_ref[...], sa_ref[...], sb_ref[...]
    quarter = HEAD_DIM // 4

    def prep(x, g, scale):
        ms = jnp.mean(x * x, axis=-1, keepdims=True)
        y = x * lax.rsqrt(ms + EPS) * g
        r = y * c + pltpu.roll(y, HEAD_DIM - quarter, 1) * sa + pltpu.roll(y, quarter, 1) * sb
        return r * scale

    for h in range(N_Q_HEADS):
        sl = slice(h * HEAD_DIM, (h + 1) * HEAD_DIM)
        qo_ref[:, sl] = prep(q_ref[:, sl].astype(F32), gq_ref[...], Q_SCALE).astype(qo_ref.dtype)
    for h in range(N_KV_HEADS):
        sl = slice(h * HEAD_DIM, (h + 1) * HEAD_DIM)
        ko_ref[:, sl] = prep(k_ref[:, sl].astype(F32), gk_ref[...], 1.0).astype(ko_ref.dtype)


def _qk_prep(proj, gq, gk, seq, *, tm):
    t = proj.shape[0]
    tables = _rope_tables(seq)
    per_seq = seq // tm
    tab_spec = pl.BlockSpec((tm, HEAD_DIM), lambda i: (i % per_seq, 0))
    g_spec = pl.BlockSpec((1, HEAD_DIM), lambda i: (0, 0))
    return pl.pallas_call(
        _qk_kernel,
        grid=(t // tm,),
        in_specs=[
            pl.BlockSpec((tm, ATTN_WIDTH), lambda i: (i, COL_Q // ATTN_WIDTH)),
            pl.BlockSpec((tm, KV_WIDTH), lambda i: (i, COL_K // KV_WIDTH)),
            tab_spec, tab_spec, tab_spec, g_spec, g_spec,
        ],
        out_specs=[pl.BlockSpec((tm, ATTN_WIDTH), lambda i: (i, 0)), pl.BlockSpec((tm, KV_WIDTH), lambda i: (i, 0))],
        out_shape=[jax.ShapeDtypeStruct((t, ATTN_WIDTH), BF16), jax.ShapeDtypeStruct((t, KV_WIDTH), BF16)],
        compiler_params=_params("parallel"),
        name="qk_prep",
    )(proj, proj, *tables, gq.reshape(1, HEAD_DIM), gk.reshape(1, HEAD_DIM))


def _attn_kernel(q_ref, k_ref, v_ref, o_ref, *, tk):
    tq = q_ref.shape[0]
    n_chunks = k_ref.shape[0] // tk
    ones = jnp.ones((tk, HEAD_DIM), v_ref.dtype)
    for h in range(GQA_GROUP):
        sl = slice(h * HEAD_DIM, (h + 1) * HEAD_DIM)
        q = q_ref[:, sl]

        def body(i, carry, q=q):
            m, acc = carry
            off = pl.multiple_of(i * tk, tk)
            kc = k_ref[pl.ds(off, tk), :]
            vc = jnp.concatenate([v_ref[pl.ds(off, tk), :], ones], axis=1)
            s = lax.dot_general(q, kc, (((1,), (1,)), ((), ())), preferred_element_type=F32)
            m_new = jnp.maximum(m, jnp.max(s, axis=-1, keepdims=True))
            alpha = jnp.exp2(m - m_new)
            p = jnp.exp2(s - m_new)
            acc = alpha * acc + jnp.dot(p.astype(BF16), vc, preferred_element_type=F32)
            return m_new, acc

        init = (jnp.full((tq, 1), -jnp.inf, F32), jnp.zeros((tq, 2 * HEAD_DIM), F32))
        _, acc = lax.fori_loop(0, n_chunks, body, init, unroll=True)
        o_ref[:, sl] = (acc[:, :HEAD_DIM] / acc[:, HEAD_DIM:]).astype(o_ref.dtype)


def _attention(q, k, proj, batch, seq, *, tq, tk):
    t = q.shape[0]
    nq = seq // tq
    group_w = GQA_GROUP * HEAD_DIM
    return pl.pallas_call(
        functools.partial(_attn_kernel, tk=tk),
        grid=(batch, N_KV_HEADS, nq),
        in_specs=[
            pl.BlockSpec((tq, group_w), lambda b, g, i: (b * nq + i, g)),
            pl.BlockSpec((seq, HEAD_DIM), lambda b, g, i: (b, g)),
            pl.BlockSpec((seq, HEAD_DIM), lambda b, g, i: (b, COL_V // HEAD_DIM + g)),
        ],
        out_specs=pl.BlockSpec((tq, group_w), lambda b, g, i: (b * nq + i, g)),
        out_shape=jax.ShapeDtypeStruct((t, ATTN_WIDTH), BF16),
        compiler_params=_params("parallel", "parallel", "parallel"),
        name="attention",
    )(q, k, proj)


def _mix_kernel(attn_ref, cx_ref, gb_ref, gc_ref, cxp_ref, gcp_ref, cxn_ref, gcn_ref, w_ref, ga_ref, gv_ref,
                o_ref, *, per_seq):
    i = pl.program_id(0)
    tm = cx_ref.shape[0]
    u = gc_ref[...].astype(F32) * cx_ref[...].astype(F32)
    last_row = BF16_SUBLANES - 1
    prev = gcp_ref[last_row:, :].astype(F32) * cxp_ref[last_row:, :].astype(F32)
    nxt = gcn_ref[0:1, :].astype(F32) * cxn_ref[0:1, :].astype(F32)
    pos = i % per_seq
    prev = jnp.where(pos == 0, 0.0, prev)
    nxt = jnp.where(pos == per_seq - 1, 0.0, nxt)
    row = lax.broadcasted_iota(jnp.int32, (tm, 1), 0)
    u_dn = jnp.where(row == 0, prev, pltpu.roll(u, 1, 0))
    u_up = jnp.where(row == tm - 1, nxt, pltpu.roll(u, tm - 1, 0))
    w = w_ref[...]
    conv = gb_ref[...].astype(F32) * (u_dn * w[0:1] + u * w[1:2] + u_up * w[2:3])

    def norm(z, g):
        ms = jnp.mean(z * z, axis=-1, keepdims=True)
        return z * lax.rsqrt(ms + EPS) * g

    o_ref[:, :ATTN_WIDTH] = norm(attn_ref[...].astype(F32), ga_ref[...]).astype(o_ref.dtype)
    o_ref[:, ATTN_WIDTH:] = norm(conv, gv_ref[...]).astype(o_ref.dtype)


def _mix_norm(attn, proj, conv_w, ga, gv, seq, *, tm):
    t = attn.shape[0]
    per_seq = seq // tm
    halo = BF16_SUBLANES
    n_halo = t // halo
    cw = CONV_WIDTH

    def main(col):
        return pl.BlockSpec((tm, cw), lambda i: (i, col // cw))

    def prev(col):
        return pl.BlockSpec((halo, cw), lambda i: (jnp.maximum(i * (tm // halo) - 1, 0), col // cw))

    def nxt(col):
        return pl.BlockSpec((halo, cw), lambda i: (jnp.minimum((i + 1) * (tm // halo), n_halo - 1), col // cw))

    def vec(n):
        return pl.BlockSpec((n, cw), lambda i: (0, 0))

    return pl.pallas_call(
        functools.partial(_mix_kernel, per_seq=per_seq),
        grid=(t // tm,),
        in_specs=[pl.BlockSpec((tm, ATTN_WIDTH), lambda i: (i, 0)), main(COL_CX), main(COL_GB), main(COL_GC),
                  prev(COL_CX), prev(COL_GC), nxt(COL_CX), nxt(COL_GC), vec(3), vec(1), vec(1)],
        out_specs=pl.BlockSpec((tm, ATTN_WIDTH + cw), lambda i: (i, 0)),
        out_shape=jax.ShapeDtypeStruct((t, ATTN_WIDTH + cw), BF16),
        compiler_params=_params("parallel"),
        name="mix_norm",
    )(attn, proj, proj, proj, proj, proj, proj, proj, conv_w, ga.reshape(1, -1), gv.reshape(1, -1))


def _peer_scores_kernel(wq_ref, h_ref, sk_ref, a_ref, b_ref):
    qt = jnp.dot(wq_ref[...], h_ref[...], preferred_element_type=F32)

    def split(v):
        hi = v.astype(BF16)
        return hi, (v - hi.astype(F32)).astype(BF16)

    q_hi, q_lo = split(qt)
    for h in range(PEER_HEADS):
        for p, o_ref in enumerate((a_ref, b_ref)):
            rows = slice((2 * h + p) * N_KEYS, (2 * h + p + 1) * N_KEYS)
            k_hi, k_lo = split(sk_ref[h, p])
            lhs = jnp.concatenate([k_hi, k_hi, k_lo], axis=1)
            rhs = jnp.concatenate([q_hi[rows, :], q_lo[rows, :], q_hi[rows, :]], axis=0)
            o_ref[h * N_KEYS:(h + 1) * N_KEYS, :] = jnp.dot(lhs, rhs, preferred_element_type=F32)


def _peer_scores(wq_t, h_t, sub_keys, *, tn):
    rows, d = wq_t.shape
    t = h_t.shape[1]
    out = jax.ShapeDtypeStruct((PEER_HEADS * N_KEYS, t), F32)
    once = pl.Buffered(1)
    return pl.pallas_call(
        _peer_scores_kernel,
        grid=(t // tn,),
        in_specs=[
            pl.BlockSpec((rows, d), lambda i: (0, 0), pipeline_mode=once),
            pl.BlockSpec((d, tn), lambda i: (0, i)),
            pl.BlockSpec(sub_keys.shape, lambda i: (0, 0, 0, 0), pipeline_mode=once),
        ],
        out_specs=[pl.BlockSpec((PEER_HEADS * N_KEYS, tn), lambda i: (0, i))] * 2,
        out_shape=[out, out],
        compiler_params=_params("parallel"),
        name="peer_scores",
    )(wq_t, h_t, sub_keys)


def _top_sorted(work, n):
    out = []
    for _ in range(n):
        m = jnp.max(work, axis=0, keepdims=True)
        out.append(m)
        work = jnp.where(work == m, -jnp.inf, work)
    return out


SELECT_TOP_ROWS = 24
UP_CHUNKS = 4
K_CHUNKS = 8
GATE_ROWS = 16


def _peer_select_kernel(a_ref, b_ref, thr_ref, ea_ref, eb_ref, a_top, b_top):
    k = PEER_TOPK
    sub = 8
    a_top[...] = jnp.full(a_top.shape, -jnp.inf, F32)
    b_top[...] = jnp.full(b_top.shape, -jnp.inf, F32)
    for h in range(PEER_HEADS):
        rows = slice(h * N_KEYS, (h + 1) * N_KEYS)
        a = a_ref[rows, :]
        b = b_ref[rows, :]
        a_sorted = _top_sorted(a, k + 1)
        b_sorted = _top_sorted(b, k + 1)
        for r in range(k + 1):
            a_top[r:r + 1, :] = a_sorted[r]
            b_top[r:r + 1, :] = b_sorted[r]
        slabs = [a_sorted[0] + b_top[...]]
        for r in range(1, sub):
            slabs.append(a_sorted[r] + b_top[0:sub, :])
        slabs.append(a_top[sub:, :] + b_sorted[0])
        top = _top_sorted(jnp.concatenate(slabs, axis=0), k + 1)
        z = jnp.ones_like(top[0])
        for r in range(1, k):
            z = z + jnp.exp(top[r] - top[0])
        tau = 0.5 * (top[k - 1] + top[k])
        thr_ref[:, h, :] = tau - a
        ea_ref[:, h, :] = jnp.exp(a - a_sorted[0])
        eb_ref[rows, :] = jnp.exp(b - b_sorted[0]) / z


def _peer_select(a_t, b_t, *, tn):
    rows, t = a_t.shape
    spec = pl.BlockSpec((rows, tn), lambda i: (0, i))
    out = jax.ShapeDtypeStruct((rows, t), F32)
    key_spec = pl.BlockSpec((N_KEYS, PEER_HEADS, tn), lambda i: (0, 0, i))
    key_out = jax.ShapeDtypeStruct((N_KEYS, PEER_HEADS, t), F32)
    return pl.pallas_call(
        _peer_select_kernel,
        grid=(t // tn,),
        in_specs=[spec, spec],
        out_specs=[key_spec, key_spec, spec],
        out_shape=[key_out, key_out, out],
        scratch_shapes=[pltpu.VMEM((SELECT_TOP_ROWS, tn), F32), pltpu.VMEM((SELECT_TOP_ROWS, tn), F32)],
        compiler_params=_params("parallel"),
        name="peer_select",
    )(a_t, b_t)


def _peer_kernel(h_ref, wd_ref, wu_ref, thr_ref, ea_ref, b_ref, eb_ref, out_ref, s_buf0, s_buf1, a_buf0, a_buf1,
                 o_ref, *, n_blocks, n_exp_blocks):
    s = pl.program_id(0)
    n_slabs = wd_ref.shape[0] // N_KEYS

    @pl.when(s == 0)
    def _():
        for buf in (s_buf0, s_buf1, a_buf0, a_buf1):
            buf[...] = jnp.zeros_like(buf)

    @pl.when((jnp.maximum(s - 2, 0) % n_exp_blocks) == 0)
    def _():
        o_ref[...] = jnp.zeros_like(o_ref)

    def step(s_cur, s_prv, a_cur, a_prv):
        first_key = (jnp.clip(s - 1, 0, n_blocks - 1) % n_exp_blocks) * n_slabs
        out_rows = o_ref.shape[0] // n_slabs
        tiles_per_slab = N_KEYS // GATE_ROWS
        k_chunk = h_ref.shape[0] // K_CHUNKS

        def gate_tile(ii, r, anchor):
            key = first_key + ii
            zero_bits = (lax.bitcast_convert_type(anchor, jnp.uint32) >> 16) >> 16
            g = lax.bitcast_convert_type(zero_bits, F32)
            for h in range(PEER_HEADS):
                sub = slice(h * N_KEYS + r * GATE_ROWS, h * N_KEYS + (r + 1) * GATE_ROWS)
                thr_row = thr_ref[key, pl.ds(h, 1), :]
                ea_row = ea_ref[key, pl.ds(h, 1), :]
                below = lax.bitcast_convert_type(thr_row - b_ref[sub, :], jnp.int32)
                keep = lax.bitcast_convert_type(eb_ref[sub, :], jnp.int32) & (below >> 31)
                g = g + lax.bitcast_convert_type(keep, F32) * ea_row
            srows = slice(ii * N_KEYS + r * GATE_ROWS, ii * N_KEYS + (r + 1) * GATE_ROWS)
            sv = s_prv[srows, :]
            act = 0.5 * sv * (1.0 + lax.erf(sv * INV_SQRT2))
            a_prv[srows, :] = (act * g).astype(a_prv.dtype)

        def up_piece(ii, u):
            up_rows = out_rows // UP_CHUNKS
            orow = slice(ii * out_rows + u * up_rows, ii * out_rows + (u + 1) * up_rows)
            upd = jnp.dot(wu_ref[orow, :], a_cur[...], preferred_element_type=F32)
            o_ref[orow, :] += upd
            return upd

        def down_piece(ii, q):
            pair = slice((ii // 2) * 2 * N_KEYS, ((ii // 2) * 2 + 2) * N_KEYS)
            kc = (ii % 2) * (K_CHUNKS // 2) + q
            ks = slice(kc * k_chunk, (kc + 1) * k_chunk)
            part = jnp.dot(wd_ref[pair, ks], h_ref[ks, :], preferred_element_type=F32)
            if kc == 0:
                s_cur[pair, :] = part
            else:
                s_cur[pair, :] += part
            return part

        for ii in range(n_slabs):
            pieces = []
            for j in range(max(UP_CHUNKS, K_CHUNKS // 2)):
                if j < UP_CHUNKS:
                    pieces.append(functools.partial(up_piece, ii, j))
                if j < K_CHUNKS // 2:
                    pieces.append(functools.partial(down_piece, ii, j))
            tiles = [[r for r in range(tiles_per_slab) if r * len(pieces) // tiles_per_slab == p]
                     for p in range(len(pieces))]
            for piece, mine in zip(pieces, tiles):
                res = piece()
                for n, r in enumerate(mine):
                    at = n * (res.shape[0] // len(mine))
                    gate_tile(ii, r, res[at:at + GATE_ROWS, :])

    pl.when(s % 2 == 0)(functools.partial(step, s_buf0, s_buf1, a_buf0, a_buf1))
    pl.when(s % 2 == 1)(functools.partial(step, s_buf1, s_buf0, a_buf1, a_buf0))

    @pl.when((s >= 2) & ((jnp.maximum(s - 2, 0) % n_exp_blocks) == n_exp_blocks - 1))
    def _():
        out_ref[...] = o_ref[...].astype(out_ref.dtype)


def _peer_mixer(h_t, wd, wu_blk, thr_t, ea_t, b_t, eb_t, *, tn):
    d, t = h_t.shape
    n_exp_blocks, _, te = wu_blk.shape
    n_blocks = (t // tn) * n_exp_blocks
    rows = b_t.shape[0]

    def blk(s, lag):
        return jnp.clip(s - lag, 0, n_blocks - 1)

    once = pl.Buffered(1)
    tok_spec = pl.BlockSpec((rows, tn), lambda s: (0, blk(s, 1) // n_exp_blocks), pipeline_mode=once)
    key_spec = pl.BlockSpec((N_KEYS, PEER_HEADS, tn), lambda s: (0, 0, blk(s, 1) // n_exp_blocks))
    return pl.pallas_call(
        functools.partial(_peer_kernel, n_blocks=n_blocks, n_exp_blocks=n_exp_blocks),
        grid=(n_blocks + 2,),
        in_specs=[
            pl.BlockSpec((d, tn), lambda s: (0, blk(s, 0) // n_exp_blocks)),
            pl.BlockSpec((te, d), lambda s: (blk(s, 0) % n_exp_blocks, 0)),
            pl.BlockSpec((None, d, te), lambda s: (blk(s, 2) % n_exp_blocks, 0, 0)),
            key_spec, key_spec, tok_spec, tok_spec,
        ],
        out_specs=pl.BlockSpec((d, tn), lambda s: (0, blk(s, 2) // n_exp_blocks)),
        out_shape=jax.ShapeDtypeStruct((d, t), BF16),
        scratch_shapes=[pltpu.VMEM((te, tn), F32), pltpu.VMEM((te, tn), F32),
                        pltpu.VMEM((te, tn), BF16), pltpu.VMEM((te, tn), BF16), pltpu.VMEM((d, tn), F32)],
        compiler_params=_params("arbitrary"),
        name="peer_mixer",
    )(h_t, wd, wu_blk, thr_t, ea_t, b_t, eb_t)


def _final_kernel(x_ref, p_ref, g_ref, o_ref):
    x = x_ref[...] + p_ref[...].astype(F32).T
    ms = jnp.mean(x * x, axis=-1, keepdims=True)
    o_ref[...] = x * lax.rsqrt(ms + EPS) * g_ref[...]


def _final(x1, peer_t, g, *, tm):
    t, d = x1.shape
    return pl.pallas_call(
        _final_kernel,
        grid=(t // tm,),
        in_specs=[pl.BlockSpec((tm, d), lambda i: (i, 0)), pl.BlockSpec((d, tm), lambda i: (0, i)),
                  pl.BlockSpec((1, d), lambda i: (0, 0))],
        out_specs=pl.BlockSpec((tm, d), lambda i: (i, 0)),
        out_shape=jax.ShapeDtypeStruct((t, d), F32),
        compiler_params=_params("parallel"),
        name="final_norm",
    )(x1, peer_t, g.reshape(1, d))


def _tiles(t, seq):
    return dict(
        norm_tm=min(256, t),
        mm_tm=min(1024, t),
        mm_tn=1024,
        qk_tm=min(256, seq),
        attn_tq=min(512, seq),
        attn_tk=min(1024, seq),
        mix_tm=min(256, seq),
        score_tn=min(512, t),
        select_tn=min(256, t),
        peer_tn=min(512, t),
        peer_te=512,
    )


def kernel(x, norm_mix_g, w_in, q_norm_g, k_norm_g, conv_w, attn_out_g, conv_out_g, w_out, norm_ffn_g,
           peer_w_query, peer_sub_keys, peer_w_down, peer_w_up, norm_final_g):
    batch, seq, d = x.shape
    assert w_in.shape[0] == 1, "one layer: the final norm is fused with the layer's PEER residual"
    t = batch * seq
    tl = _tiles(t, seq)
    xf = x.reshape(t, d)

    w_in_b = w_in[0].astype(BF16)
    w_out_b = w_out[0].astype(BF16)
    wq_t = _cast_transpose(peer_w_query[0], tm=tl["norm_tm"])
    wd_b = peer_w_down[0].astype(BF16)
    wu_blk = _cast_transpose(peer_w_up[0], tm=tl["peer_te"], blocked=True)

    h = _rmsnorm(xf, norm_mix_g[0], tm=tl["norm_tm"])
    n_cols, n_q, n_kv = IN_COLS // tl["mm_tn"], ATTN_WIDTH // tl["mm_tn"], 2 * KV_WIDTH // tl["mm_tn"]
    assert n_q * tl["mm_tn"] == ATTN_WIDTH and n_kv * tl["mm_tn"] == 2 * KV_WIDTH

    def w_in_block(j):
        return jnp.where(j < n_q, j, jnp.where(j < n_cols - n_kv, j + n_kv, j - (n_cols - n_kv) + n_q))

    proj = _matmul(h, w_in_b, tm=tl["mm_tm"], tn=tl["mm_tn"], out_dtype=BF16, col_block=w_in_block, name="in_proj")
    q, k = _qk_prep(proj, q_norm_g[0], k_norm_g[0], seq, tm=tl["qk_tm"])
    attn = _attention(q, k, proj, batch, seq, tq=tl["attn_tq"], tk=tl["attn_tk"])
    y = _mix_norm(attn, proj, conv_w[0], attn_out_g[0], conv_out_g[0], seq, tm=tl["mix_tm"])
    x1 = _matmul(y, w_out_b, tm=tl["mm_tm"], tn=tl["mm_tn"], out_dtype=F32, residual=xf, name="out_proj")

    h2_t = _rmsnorm(x1, norm_ffn_g[0], tm=tl["norm_tm"], transpose=True)
    a_t, b_t = _peer_scores(wq_t, h2_t, peer_sub_keys[0], tn=tl["score_tn"])
    thr_t, ea_t, eb_t = _peer_select(a_t, b_t, tn=tl["select_tn"])
    peer_t = _peer_mixer(h2_t, wd_b, wu_blk, thr_t, ea_t, b_t, eb_t, tn=tl["peer_tn"])
    out = _final(x1, peer_t, norm_final_g, tm=tl["norm_tm"])
    return out.reshape(batch, seq, d)
```
